```python
import jax, jax.numpy as jnp
from jax import lax
import numpy as np

D_MODEL = 1024
BATCH = 4
SEQ = 4096
DEPTH = 2

CTX_LEN = 256
GRID_W = 64
CHUNK = 128
Q_BLOCK = 128
W_A = D_MODEL // 4
A_HEADS = 4
A_HEAD_DIM = W_A // A_HEADS
W_B = D_MODEL // 4
POOL_WINDOWS = (2, 4, 8, 16)
B_GROUP = W_B // len(POOL_WINDOWS)
V_DIM = 64
W_C = D_MODEL // 2
MLA_HEADS = W_C // V_DIM
QK_NOPE = 64
QK_ROPE = 32
QK_HEAD = QK_NOPE + QK_ROPE
Q_RANK = 256
KV_RANK = 128
D_MIX = W_A + W_B + W_C
SPLIT_POINTS = (2 * W_A, 2 * W_A + W_B, 2 * W_A + W_B + Q_RANK, 2 * W_A + W_B + Q_RANK + KV_RANK)
D_IN = 2 * W_A + W_B + Q_RANK + KV_RANK + QK_ROPE
D_FF = -(-8 * D_MODEL // (3 * 256)) * 256
ROPE_BASE = 10000.0
EPS = 1e-6
N_MOD = 6

kernel_name = "hybrid_parallel_groups_dit_block"


def rms_norm(x, g):
    xf = x.astype(jnp.float32)
    y = xf * lax.rsqrt(jnp.mean(xf * xf, axis=-1, keepdims=True) + EPS)
    return (y * g.astype(jnp.float32)).astype(x.dtype)


def axial_rope_tables(n_tokens):
    rows = n_tokens // GRID_W
    row = jnp.repeat(jnp.arange(rows), GRID_W).astype(jnp.float32)
    col = jnp.tile(jnp.arange(GRID_W), rows).astype(jnp.float32)
    n_freq = QK_ROPE // 4
    inv = ROPE_BASE ** (-jnp.arange(n_freq, dtype=jnp.float32) / n_freq)
    ang = jnp.concatenate([row[:, None] * inv, col[:, None] * inv], axis=-1)
    return jnp.cos(ang), jnp.sin(ang)


def apply_rope(x, cos, sin):
    shp = x.shape
    xp = x.astype(jnp.float32).reshape(shp[:-1] + (QK_ROPE // 2, 2))
    x1, x2 = xp[..., 0], xp[..., 1]
    c = cos[None, :, None, :]
    s = sin[None, :, None, :]
    out = jnp.stack([x1 * c - x2 * s, x1 * s + x2 * c], axis=-1)
    return out.reshape(shp).astype(x.dtype)


def chunk_mlp(z, norm_g, w_spatial, b_spatial):
    bn, n_tok = z.shape[0], z.shape[1]
    u, v = z[..., :W_A], z[..., W_A:]
    v = rms_norm(v, norm_g).reshape(bn, n_tok // CHUNK, CHUNK, A_HEADS, A_HEAD_DIM)
    mixed = jnp.einsum('hij,bnjhd->bnihd', w_spatial, v) + b_spatial.T[:, :, None]
    return u * mixed.reshape(bn, n_tok, W_A)


def multiscale_pool(z, w_pool, pool_scale):
    n_tok = z.shape[1]
    zf = z.astype(jnp.float32)
    cs = jnp.concatenate([jnp.zeros_like(zf[:, :1]), jnp.cumsum(zf, axis=1)], axis=1)
    t = jnp.arange(n_tok)
    outs = []
    for g, w in enumerate(POOL_WINDOWS):
        lo = jnp.clip(t - w // 2, 0, n_tok)
        hi = jnp.clip(t + w // 2, 0, n_tok)
        seg = cs[:, :, g * B_GROUP:(g + 1) * B_GROUP]
        mean = (jnp.take(seg, hi, axis=1) - jnp.take(seg, lo, axis=1)) / (hi - lo).astype(jnp.float32)[None, :, None]
        d = (mean - zf[:, :, g * B_GROUP:(g + 1) * B_GROUP]).astype(z.dtype)
        outs.append(jnp.einsum('blc,cd->bld', d, w_pool[g]))
    return jnp.concatenate(outs, axis=-1) * pool_scale


def mla_qkv(p_q, p_kv, p_kr, q_a_norm_g, w_q_b, kv_a_norm_g, w_kv_b, q_norm_g, k_norm_g, rope):
    bn, n_tok = p_q.shape[0], p_q.shape[1]
    q = (rms_norm(p_q, q_a_norm_g) @ w_q_b).reshape(bn, n_tok, MLA_HEADS, QK_HEAD)
    kv = (rms_norm(p_kv, kv_a_norm_g) @ w_kv_b).reshape(bn, n_tok, MLA_HEADS, QK_NOPE + V_DIM)
    k_nope, v = kv[..., :QK_NOPE], kv[..., QK_NOPE:]
    k_rope = jnp.broadcast_to(p_kr[:, :, None, :], (bn, n_tok, MLA_HEADS, QK_ROPE))
    k = jnp.concatenate([k_nope, k_rope], axis=-1)
    q = rms_norm(q, q_norm_g)
    k = rms_norm(k, k_norm_g)
    if rope is not None:
        cos, sin = rope
        q = jnp.concatenate([q[..., :QK_NOPE], apply_rope(q[..., QK_NOPE:], cos, sin)], axis=-1)
        k = jnp.concatenate([k[..., :QK_NOPE], apply_rope(k[..., QK_NOPE:], cos, sin)], axis=-1)
    return q, k, v


def attend_blocks(q, k, v):
    bn, n_tok = q.shape[0], q.shape[1]
    nb = n_tok // Q_BLOCK
    scale = QK_HEAD ** -0.5
    qb = q.reshape(bn, nb, Q_BLOCK, MLA_HEADS, QK_HEAD).transpose(1, 0, 2, 3, 4)

    def one_block(qblk):
        s = jnp.einsum('bqhd,bkhd->bhqk', qblk, k, preferred_element_type=jnp.float32) * scale
        p = jax.nn.softmax(s, axis=-1)
        return jnp.einsum('bhqk,bkhd->bqhd', p.astype(v.dtype), v)

    o = lax.map(one_block, qb)
    return o.transpose(1, 0, 2, 3, 4).reshape(bn, n_tok, MLA_HEADS * V_DIM)


def swiglu(h, w_gate_up, w_down):
    gu = h @ w_gate_up
    return (jax.nn.silu(gu[..., :D_FF]) * gu[..., D_FF:]) @ w_down


def hybrid_layer(x, xc, s_lat, s_ctx, rope, norm1_g, norm2_g, w_ada, b_ada, w_in,
                 sgu_norm_g, w_spatial, b_spatial, w_pool, pool_scale,
                 q_a_norm_g, w_q_b, kv_a_norm_g, w_kv_b, q_norm_g, k_norm_g,
                 w_out, w_gate_up, w_down, update_ctx):
    bn = x.shape[0]
    mod = (s_lat @ w_ada + b_ada).reshape(bn, N_MOD, 1, D_MODEL)
    modc = (s_ctx @ w_ada + b_ada).reshape(N_MOD, 1, 1, D_MODEL)
    sh1, sc1, g1, sh2, sc2, g2 = [mod[:, i] for i in range(N_MOD)]
    sh1c, sc1c, g1c, sh2c, sc2c, g2c = [modc[i] for i in range(N_MOD)]

    h = rms_norm(x, norm1_g) * (1 + sc1) + sh1
    hc = rms_norm(xc, norm1_g) * (1 + sc1c) + sh1c
    a, b, pq, pkv, pkr = jnp.split(h @ w_in, SPLIT_POINTS, axis=-1)
    ac, bc, pqc, pkvc, pkrc = jnp.split(hc @ w_in, SPLIT_POINTS, axis=-1)

    y_a = chunk_mlp(jax.nn.gelu(a, approximate=False), sgu_norm_g, w_spatial, b_spatial)
    y_b = multiscale_pool(b, w_pool, pool_scale)
    q, k, v = mla_qkv(pq, pkv, pkr, q_a_norm_g, w_q_b, kv_a_norm_g, w_kv_b, q_norm_g, k_norm_g, rope)
    qc, kc, vc = mla_qkv(pqc, pkvc, pkrc, q_a_norm_g, w_q_b, kv_a_norm_g, w_kv_b, q_norm_g, k_norm_g, None)
    y_c = attend_blocks(q, jnp.concatenate([kc, k], axis=1), jnp.concatenate([vc, v], axis=1))
    x = x + g1 * (jnp.concatenate([y_a, y_b, y_c], axis=-1) @ w_out)

    h2 = rms_norm(x, norm2_g) * (1 + sc2) + sh2
    x = x + g2 * swiglu(h2, w_gate_up, w_down)

    if update_ctx:
        y_ac = chunk_mlp(jax.nn.gelu(ac, approximate=False), sgu_norm_g, w_spatial, b_spatial)
        y_bc = multiscale_pool(bc, w_pool, pool_scale)
        y_cc = attend_blocks(qc, kc, vc)
        xc = xc + g1c * (jnp.concatenate([y_ac, y_bc, y_cc], axis=-1) @ w_out)
        h2c = rms_norm(xc, norm2_g) * (1 + sc2c) + sh2c
        xc = xc + g2c * swiglu(h2c, w_gate_up, w_down)
    return x, xc


def setup_inputs(seed: int = 0) -> dict:
    key = jax.random.key(seed)
    ks = jax.random.split(key, 28)
    f32 = jnp.float32

    def nrm(k, shape, scale):
        return jax.random.normal(k, shape, f32) * scale

    def gain(k, shape):
        return 1.0 + 0.02 * jax.random.normal(k, shape, f32)

    L = DEPTH
    return {
        "x": nrm(ks[0], (BATCH, SEQ, D_MODEL), 1.0),
        "c": nrm(ks[1], (BATCH, D_MODEL), 1.0),
        "ctx": nrm(ks[2], (BATCH, CTX_LEN, D_MODEL), 1.0),
        "c_ctx": nrm(ks[3], (D_MODEL,), 1.0),
        "norm1_g": gain(ks[4], (L, D_MODEL)),
        "norm2_g": gain(ks[5], (L, D_MODEL)),
        "w_ada": nrm(ks[6], (L, D_MODEL, N_MOD * D_MODEL), 0.5 * D_MODEL ** -0.5),
        "b_ada": nrm(ks[7], (L, N_MOD * D_MODEL), 0.02),
        "w_in": nrm(ks[8], (L, D_MODEL, D_IN), D_MODEL ** -0.5),
        "sgu_norm_g": gain(ks[9], (L, W_A)),
        "w_spatial": nrm(ks[10], (L, A_HEADS, CHUNK, CHUNK), CHUNK ** -0.5),
        "b_spatial": gain(ks[11], (L, A_HEADS, CHUNK)),
        "w_pool": nrm(ks[12], (L, len(POOL_WINDOWS), B_GROUP, B_GROUP), B_GROUP ** -0.5),
        "pool_scale": gain(ks[13], (L, W_B)),
        "q_a_norm_g": gain(ks[14], (L, Q_RANK)),
        "w_q_b": nrm(ks[15], (L, Q_RANK, MLA_HEADS * QK_HEAD), Q_RANK ** -0.5),
        "kv_a_norm_g": gain(ks[16], (L, KV_RANK)),
        "w_kv_b": nrm(ks[17], (L, KV_RANK, MLA_HEADS * (QK_NOPE + V_DIM)), KV_RANK ** -0.5),
        "q_norm_g": gain(ks[18], (L, QK_HEAD)),
        "k_norm_g": gain(ks[19], (L, QK_HEAD)),
        "w_out": nrm(ks[20], (L, D_MIX, D_MODEL), D_MIX ** -0.5),
        "w_gate_up": nrm(ks[21], (L, D_MODEL, 2 * D_FF), D_MODEL ** -0.5),
        "w_down": nrm(ks[22], (L, D_FF, D_MODEL), D_FF ** -0.5),
    }


def reference(x, c, ctx, c_ctx, norm1_g, norm2_g, w_ada, b_ada, w_in, sgu_norm_g,
              w_spatial, b_spatial, w_pool, pool_scale, q_a_norm_g, w_q_b,
              kv_a_norm_g, w_kv_b, q_norm_g, k_norm_g, w_out, w_gate_up, w_down):
    n_tok = x.shape[1]
    rope = axial_rope_tables(n_tok)
    s_lat = jax.nn.silu(c)
    s_ctx = jax.nn.silu(c_ctx)
    xc = ctx
    for i in range(DEPTH):
        x, xc = hybrid_layer(
            x, xc, s_lat, s_ctx, rope, norm1_g[i], norm2_g[i], w_ada[i], b_ada[i], w_in[i],
            sgu_norm_g[i], w_spatial[i], b_spatial[i], w_pool[i], pool_scale[i],
            q_a_norm_g[i], w_q_b[i], kv_a_norm_g[i], w_kv_b[i], q_norm_g[i], k_norm_g[i],
            w_out[i], w_gate_up[i], w_down[i], update_ctx=(i < DEPTH - 1))
    return x
```

```python
import functools

import jax
import jax.numpy as jnp
import numpy as np
from jax import lax
from jax.experimental import pallas as pl
from jax.experimental.pallas import tpu as pltpu

F32 = jnp.float32
BF16 = jnp.bfloat16

GRID_W = 64
CHUNK = 128
A_HEADS = 4
POOL_WINDOWS = (2, 4, 8, 16)
V_DIM = 64
QK_NOPE = 64
QK_ROPE = 32
QK_HEAD = QK_NOPE + QK_ROPE
Q_RANK = 256
KV_RANK = 128
ROPE_BASE = 10000.0
EPS = 1e-6
N_MOD = 6

LANES = 128
SUBLANES = 8
HALO = 8
VMEM_LIMIT = 52 * 1024 * 1024

TM = 256
FF_CHUNK = 256


def _rms(x, g):
    return x * lax.rsqrt(jnp.mean(x * x, axis=-1, keepdims=True) + EPS) * g


def _params(*sem):
    return pltpu.CompilerParams(dimension_semantics=sem, vmem_limit_bytes=VMEM_LIMIT)


def _const_spec(shape):
    nd = len(shape)
    return pl.BlockSpec(shape, lambda *_: (0,) * nd)


def _mod_kernel(c_ref, w_ref, b_ref, o_ref):
    c = c_ref[...]
    s = c * jax.nn.sigmoid(c)
    o_ref[...] = jnp.dot(s, w_ref[...], preferred_element_type=F32) + b_ref[...]


def _mod_call(cc, w_ada, b_ada):
    n_layer, d, n_out = w_ada.shape
    r = cc.shape[0]
    tn = 1024
    return pl.pallas_call(
        _mod_kernel,
        grid=(n_layer, n_out // tn),
        in_specs=[
            pl.BlockSpec((r, d), lambda l, j: (0, 0)),
            pl.BlockSpec((None, d, tn), lambda l, j: (l, 0, j)),
            pl.BlockSpec((None, 1, tn), lambda l, j: (l, 0, j)),
        ],
        out_specs=pl.BlockSpec((None, r, tn), lambda l, j: (l, 0, j)),
        out_shape=jax.ShapeDtypeStruct((n_layer, r, n_out), F32),
        compiler_params=_params("parallel", "parallel"),
        name="mod",
    )(cc, w_ada, b_ada.reshape(n_layer, 1, n_out))


def _swap_pairs(x):
    lane = lax.broadcasted_iota(jnp.int32, x.shape, 1)
    nxt = pltpu.roll(x, x.shape[1] - 1, 1)
    prv = pltpu.roll(x, 1, 1)
    return jnp.where(lane % 2 == 0, nxt, prv)


def _pre_kernel(x_ref, mod_ref, g1_ref, w_in_ref, sgu_g_ref, w_sp_ref, b_sp_ref,
                qa_g_ref, w_qb_ref, kva_g_ref, w_kvb_ref, qg_ref, kg_ref, vone_ref,
                cos_ref, sin_ref,
                ya_ref, b_ref, q_ref, k_ref, v_ref, *, w_a, w_b, n_heads):
    x = x_ref[...]
    sh1 = mod_ref[0:1, :]
    sc1 = mod_ref[1:2, :]
    h = _rms(x, g1_ref[...]) * (1.0 + sc1) + sh1
    p = jnp.dot(h.astype(BF16), w_in_ref[...], preferred_element_type=F32)

    a = p[:, :2 * w_a]
    ga = 0.5 * a * (1.0 + lax.erf(a * np.float32(1.0 / np.sqrt(2.0))))
    u = ga[:, :w_a]
    vg = _rms(ga[:, w_a:], sgu_g_ref[...]).astype(BF16)
    head_of_lane = lax.broadcasted_iota(jnp.int32, (CHUNK, w_a), 1) // (w_a // A_HEADS)
    for c in range(x.shape[0] // CHUNK):
        rows = slice(c * CHUNK, (c + 1) * CHUNK)
        r = jnp.dot(w_sp_ref[...], vg[rows, :], preferred_element_type=F32)
        mixed = r[0:CHUNK, :]
        for hh in range(1, A_HEADS):
            mixed = jnp.where(head_of_lane == hh, r[hh * CHUNK:(hh + 1) * CHUNK, :], mixed)
        ya_ref[rows, :] = (u[rows, :] * (mixed + b_sp_ref[...])).astype(ya_ref.dtype)

    o = 2 * w_a
    b_ref[...] = p[:, o:o + w_b]

    o += w_b
    pqn = _rms(p[:, o:o + Q_RANK], qa_g_ref[...]).astype(BF16)
    o += Q_RANK
    pkvn = _rms(p[:, o:o + KV_RANK], kva_g_ref[...]).astype(BF16)
    o += KV_RANK
    kr = p[:, o:o + LANES]
    q_raw = jnp.dot(pqn, w_qb_ref[...], preferred_element_type=F32)
    kv_raw = jnp.dot(pkvn, w_kvb_ref[...], preferred_element_type=F32)
    cos = cos_ref[...]
    sin = sin_ref[...]
    hw = n_heads * LANES
    for hh in range(n_heads):
        cols = slice(hh * LANES, (hh + 1) * LANES)
        qh = q_raw[:, cols]
        qn = qh * lax.rsqrt(jnp.sum(qh * qh, axis=-1, keepdims=True) * (1.0 / QK_HEAD) + EPS) * qg_ref[...]
        q_ref[:, cols] = (qn * cos + _swap_pairs(qn) * sin).astype(q_ref.dtype)
        kh = kv_raw[:, cols] + kr
        kn = kh * lax.rsqrt(jnp.sum(kh * kh, axis=-1, keepdims=True) * (1.0 / QK_HEAD) + EPS) * kg_ref[...]
        k_ref[:, cols] = (kn * cos + _swap_pairs(kn) * sin).astype(k_ref.dtype)
        v_ref[:, cols] = (kv_raw[:, hw + hh * LANES:hw + (hh + 1) * LANES] + vone_ref[...]).astype(v_ref.dtype)


def _pre_call(x_all, mod, layer, wts, cos_tab, sin_tab, *, n_lat_tiles, tiles_per_seq, n_batch):
    nt, d = x_all.shape
    w_a = wts["sgu_g"].shape[1]
    w_b = wts["pool_scale"].shape[1]
    n_heads = wts["w_qb"].shape[1] // LANES
    hw = n_heads * LANES

    def midx(i):
        return jnp.where(i < n_lat_tiles, i // tiles_per_seq, n_batch)

    def ridx(i):
        return jnp.where(i < n_lat_tiles, i % tiles_per_seq, tiles_per_seq)

    tok = lambda w: pl.BlockSpec((TM, w), lambda i: (i, 0))
    consts = [wts[k] for k in ("g1", "w_in", "sgu_g", "w_sp", "b_sp", "qa_g", "w_qb", "kva_g",
                               "w_kvb", "qg", "kg", "vone")]
    in_specs = [tok(d), pl.BlockSpec((None, None, N_MOD, d), lambda i: (layer, midx(i), 0, 0))]
    in_specs += [_const_spec(c.shape) for c in consts]
    in_specs += [pl.BlockSpec((TM, LANES), lambda i: (ridx(i), 0))] * 2
    out_shape = [jax.ShapeDtypeStruct((nt, w_a), BF16), jax.ShapeDtypeStruct((nt, w_b), F32)]
    out_shape += [jax.ShapeDtypeStruct((nt, hw), BF16)] * 3
    return pl.pallas_call(
        functools.partial(_pre_kernel, w_a=w_a, w_b=w_b, n_heads=n_heads),
        grid=(nt // TM,),
        in_specs=in_specs,
        out_specs=[tok(w_a), tok(w_b), tok(hw), tok(hw), tok(hw)],
        out_shape=out_shape,
        compiler_params=_params("parallel"),
        name="pre",
    )(x_all, mod, *consts, cos_tab, sin_tab)


def _attend(q_ref, kv_refs, o_ref, heads_per_step):
    outs = []
    for hh in range(heads_per_step):
        cols = slice(hh * LANES, (hh + 1) * LANES)
        q = q_ref[:, cols]
        scores = [lax.dot_general(q, k_ref[:, cols], (((1,), (1,)), ((), ())),
                                  preferred_element_type=F32) for k_ref, _ in kv_refs]
        m = functools.reduce(jnp.maximum, [jnp.max(s, axis=-1, keepdims=True) for s in scores])
        acc = None
        for s, (_, v_ref) in zip(scores, kv_refs):
            pv = jnp.dot(jnp.exp(s - m).astype(BF16), v_ref[:, cols], preferred_element_type=F32)
            acc = pv if acc is None else acc + pv
        outs.append(acc / acc[:, V_DIM:V_DIM + 1])
    lane = lax.broadcasted_iota(jnp.int32, outs[0].shape, 1)
    packed = jnp.where(lane < V_DIM, outs[0], pltpu.roll(outs[1], V_DIM, 1))
    o_ref[...] = packed.astype(o_ref.dtype)


def _attn_kernel(q_ref, kl_ref, vl_ref, kc_ref, vc_ref, o_ref, *, nq, heads_per_step, with_ctx):
    if not with_ctx:
        _attend(q_ref, [(kc_ref, vc_ref), (kl_ref, vl_ref)], o_ref, heads_per_step)
        return
    i = pl.program_id(2)

    @pl.when(i < nq)
    def _():
        _attend(q_ref, [(kc_ref, vc_ref), (kl_ref, vl_ref)], o_ref, heads_per_step)

    @pl.when(i == nq)
    def _():
        _attend(q_ref, [(kc_ref, vc_ref)], o_ref, heads_per_step)


def _attn_call(q, k, v, *, n_batch, seq, ctx_len, with_ctx):
    nt, hw = q.shape
    n_heads = hw // LANES
    heads_per_step = 2
    tq = ctx_len
    nq = seq // tq
    n_lat = n_batch * seq
    bw = heads_per_step * LANES

    def q_idx(b, h, i):
        return (jnp.where(i < nq, b * nq + i, n_lat // tq + b), h)

    lat = pl.BlockSpec((seq, bw), lambda b, h, i: (b, h))
    cx = pl.BlockSpec((ctx_len, bw), lambda b, h, i: (n_lat // ctx_len + b, h))
    return pl.pallas_call(
        functools.partial(_attn_kernel, nq=nq, heads_per_step=heads_per_step, with_ctx=with_ctx),
        grid=(n_batch, n_heads // heads_per_step, nq + (1 if with_ctx else 0)),
        in_specs=[pl.BlockSpec((tq, bw), q_idx), lat, lat, cx, cx],
        out_specs=pl.BlockSpec((tq, heads_per_step * V_DIM), q_idx),
        out_shape=jax.ShapeDtypeStruct((nt if with_ctx else n_lat, n_heads * V_DIM), BF16),
        compiler_params=_params("parallel", "parallel", "arbitrary"),
        name="attn",
    )(q, k, v, k, v)


def _post_kernel(x_ref, mod_ref, ya_ref, b_ref, bp_ref, bn_ref, o_ref, w_out_ref, w_pool_ref,
                 ps_ref, xo_ref, ext_ref, *, n_lat_tiles, tiles_per_seq, seq, ctx_len):
    i = pl.program_id(0)
    tm, w_b = b_ref.shape
    is_lat = i < n_lat_tiles
    t_in_seq = jnp.where(is_lat, i % tiles_per_seq, 0)
    n_seq = jnp.where(is_lat, seq, ctx_len)
    first = t_in_seq == 0
    last = jnp.where(is_lat, t_in_seq == tiles_per_seq - 1, True)

    b = b_ref[...]
    zero_halo = jnp.zeros((HALO, w_b), F32)
    ext_ref[0:HALO, :] = jnp.where(first, zero_halo, bp_ref[...])
    ext_ref[HALO:HALO + tm, :] = b
    ext_ref[HALO + tm:, :] = jnp.where(last, zero_halo, bn_ref[...])

    def shifted(off):
        return ext_ref[HALO + off:HALO + off + tm, :]

    pos = t_in_seq * tm + lax.broadcasted_iota(jnp.int32, (tm, w_b), 0)
    group = lax.broadcasted_iota(jnp.int32, (tm, w_b), 1) // (w_b // len(POOL_WINDOWS))
    acc = None
    win = None
    cnt = None
    done = 0
    for g, w in enumerate(POOL_WINDOWS):
        half = w // 2
        for off in list(range(-half, -done)) + list(range(done, half)):
            acc = shifted(off) if acc is None else acc + shifted(off)
        done = half
        c_w = jnp.minimum(pos + half, n_seq) - jnp.maximum(pos - half, 0)
        win = acc if win is None else jnp.where(group == g, acc, win)
        cnt = c_w if cnt is None else jnp.where(group == g, c_w, cnt)
    d = win / cnt.astype(F32) - b
    yb = jnp.dot(d.astype(BF16), w_pool_ref[...], preferred_element_type=F32) * ps_ref[...]

    y = jnp.concatenate([ya_ref[...], yb.astype(BF16), o_ref[...]], axis=-1)
    g1 = mod_ref[2:3, :]
    xo_ref[...] = x_ref[...] + g1 * jnp.dot(y, w_out_ref[...], preferred_element_type=F32)


def _post_call(x_all, mod, layer, ya, bpool, o, wts, *, n_tiles, n_lat_tiles, tiles_per_seq,
               n_batch, seq, ctx_len):
    nt, d = x_all.shape
    w_a = ya.shape[1]
    w_b = bpool.shape[1]
    w_c = o.shape[1]
    hb = TM // HALO
    last_halo = nt // HALO - 1

    def midx(i):
        return jnp.where(i < n_lat_tiles, i // tiles_per_seq, n_batch)

    tok = lambda w: pl.BlockSpec((TM, w), lambda i: (i, 0))
    return pl.pallas_call(
        functools.partial(_post_kernel, n_lat_tiles=n_lat_tiles, tiles_per_seq=tiles_per_seq,
                          seq=seq, ctx_len=ctx_len),
        grid=(n_tiles,),
        in_specs=[tok(d),
                  pl.BlockSpec((None, None, N_MOD, d), lambda i: (layer, midx(i), 0, 0)),
                  tok(w_a), tok(w_b),
                  pl.BlockSpec((HALO, w_b), lambda i: (jnp.maximum(i * hb - 1, 0), 0)),
                  pl.BlockSpec((HALO, w_b), lambda i: (jnp.minimum((i + 1) * hb, last_halo), 0)),
                  tok(w_c),
                  _const_spec(wts["w_out"].shape), _const_spec(wts["w_pool"].shape),
                  _const_spec(wts["pool_scale"].shape)],
        out_specs=tok(d),
        out_shape=jax.ShapeDtypeStruct((n_tiles * TM, d), F32),
        scratch_shapes=[pltpu.VMEM((TM + 2 * HALO, w_b), F32)],
        compiler_params=_params("parallel"),
        name="post",
    )(x_all, mod, ya, bpool, bpool, bpool, o, wts["w_out"], wts["w_pool"], wts["pool_scale"])


def _ffn_kernel(x_ref, mod_ref, g2_ref, w_gu_ref, w_dn_ref, xo_ref, acc_ref, *, n_chunks):
    x = x_ref[...]
    sh2 = mod_ref[3:4, :]
    sc2 = mod_ref[4:5, :]
    gate2 = mod_ref[5:6, :]
    h = (_rms(x, g2_ref[...]) * (1.0 + sc2) + sh2).astype(BF16)
    fc = FF_CHUNK
    for j in range(n_chunks):
        gu = jnp.dot(h, w_gu_ref[:, j * 2 * fc:(j + 1) * 2 * fc], preferred_element_type=F32)
        gt = gu[:, :fc]
        t = (gt * jax.nn.sigmoid(gt) * gu[:, fc:]).astype(BF16)
        part = jnp.dot(t, w_dn_ref[j * fc:(j + 1) * fc, :], preferred_element_type=F32)
        if j == 0:
            acc_ref[...] = part
        else:
            acc_ref[...] += part
    xo_ref[...] = x + gate2 * acc_ref[...]


def _ffn_call(x_all, mod, layer, wts, *, n_tiles, n_lat_tiles, tiles_per_seq, n_batch):
    nt, d = x_all.shape
    d_ff = wts["w_dn"].shape[0]

    def midx(i):
        return jnp.where(i < n_lat_tiles, i // tiles_per_seq, n_batch)

    tok = pl.BlockSpec((TM, d), lambda i: (i, 0))
    return pl.pallas_call(
        functools.partial(_ffn_kernel, n_chunks=d_ff // FF_CHUNK),
        grid=(n_tiles,),
        in_specs=[tok,
                  pl.BlockSpec((None, None, N_MOD, d), lambda i: (layer, midx(i), 0, 0)),
                  _const_spec(wts["g2"].shape), _const_spec(wts["w_gu"].shape),
                  _const_spec(wts["w_dn"].shape)],
        out_specs=tok,
        out_shape=jax.ShapeDtypeStruct((n_tiles * TM, d), F32),
        scratch_shapes=[pltpu.VMEM((TM, d), F32)],
        compiler_params=_params("parallel"),
        name="ffn",
    )(x_all, mod, wts["g2"], wts["w_gu"], wts["w_dn"])


def _rope_tables(seq):
    rows = seq // GRID_W
    row = jnp.repeat(jnp.arange(rows), GRID_W).astype(F32)
    col = jnp.tile(jnp.arange(GRID_W), rows).astype(F32)
    n_freq = QK_ROPE // 4
    inv = ROPE_BASE ** (-jnp.arange(n_freq, dtype=F32) / n_freq)
    ang = jnp.concatenate([row[:, None] * inv, col[:, None] * inv], axis=-1)
    cos = jnp.repeat(jnp.cos(ang), 2, axis=-1)
    sin = jnp.repeat(jnp.sin(ang), 2, axis=-1) * jnp.tile(jnp.array([-1.0, 1.0], F32), QK_ROPE // 2)
    pad = LANES - QK_HEAD
    cos_t = jnp.concatenate([jnp.ones((seq, QK_NOPE), F32), cos, jnp.zeros((seq, pad), F32)], axis=-1)
    sin_t = jnp.concatenate([jnp.zeros((seq, QK_NOPE), F32), sin, jnp.zeros((seq, pad), F32)], axis=-1)
    ident_c = jnp.concatenate([jnp.ones((TM, QK_HEAD), F32), jnp.zeros((TM, pad), F32)], axis=-1)
    ident_s = jnp.zeros((TM, LANES), F32)
    return jnp.concatenate([cos_t, ident_c], axis=0), jnp.concatenate([sin_t, ident_s], axis=0)


def _layer_weights(l, norm1_g, norm2_g, w_in, sgu_norm_g, w_spatial, b_spatial, w_pool, pool_scale,
                   q_a_norm_g, w_q_b, kv_a_norm_g, w_kv_b, q_norm_g, k_norm_g, w_out, w_gate_up,
                   w_down):
    d = w_in.shape[1]
    w_a = sgu_norm_g.shape[1]
    w_b = pool_scale.shape[1]
    n_heads = w_q_b.shape[2] // QK_HEAD
    pad = LANES - QK_HEAD
    split = 2 * w_a + w_b + Q_RANK + KV_RANK
    wi = w_in[l]
    w_in_p = jnp.concatenate([wi[:, :split], jnp.zeros((d, QK_NOPE), F32), wi[:, split:],
                              jnp.zeros((d, pad), F32)], axis=1)
    w_qb = jnp.pad(w_q_b[l].reshape(Q_RANK, n_heads, QK_HEAD), ((0, 0), (0, 0), (0, pad)))
    kvb = w_kv_b[l].reshape(KV_RANK, n_heads, QK_NOPE + V_DIM)
    w_kb = jnp.pad(kvb[:, :, :QK_NOPE], ((0, 0), (0, 0), (0, LANES - QK_NOPE)))
    w_vb = jnp.pad(kvb[:, :, QK_NOPE:], ((0, 0), (0, 0), (0, LANES - V_DIM)))
    w_kvb = jnp.concatenate([w_kb.reshape(KV_RANK, -1), w_vb.reshape(KV_RANK, -1)], axis=1)
    d_ff = w_down.shape[1]
    nch = d_ff // FF_CHUNK
    gate = w_gate_up[l][:, :d_ff].reshape(d, nch, 1, FF_CHUNK)
    up = w_gate_up[l][:, d_ff:].reshape(d, nch, 1, FF_CHUNK)
    w_gu = jnp.concatenate([gate, up], axis=2).reshape(d, 2 * d_ff)
    vone = jnp.zeros((1, LANES), F32).at[0, V_DIM].set(1.0)
    return dict(
        g1=norm1_g[l][None, :], g2=norm2_g[l][None, :],
        w_in=w_in_p.astype(BF16),
        sgu_g=sgu_norm_g[l][None, :],
        w_sp=w_spatial[l].reshape(A_HEADS * CHUNK, CHUNK).astype(BF16),
        b_sp=jnp.repeat(b_spatial[l].T, w_a // A_HEADS, axis=1),
        qa_g=q_a_norm_g[l][None, :], kva_g=kv_a_norm_g[l][None, :],
        w_qb=w_qb.reshape(Q_RANK, -1).astype(BF16), w_kvb=w_kvb.astype(BF16),
        qg=jnp.pad(q_norm_g[l] * np.float32(QK_HEAD ** -0.5), (0, pad))[None, :],
        kg=jnp.pad(k_norm_g[l], (0, pad))[None, :],
        vone=vone,
        w_out=w_out[l].astype(BF16),
        w_pool=jax.scipy.linalg.block_diag(*[w_pool[l, g] for g in range(w_pool.shape[1])]).astype(BF16),
        pool_scale=pool_scale[l][None, :],
        w_gu=w_gu.astype(BF16), w_dn=w_down[l].astype(BF16),
    )


def kernel(x, c, ctx, c_ctx, norm1_g, norm2_g, w_ada, b_ada, w_in, sgu_norm_g, w_spatial, b_spatial,
           w_pool, pool_scale, q_a_norm_g, w_q_b, kv_a_norm_g, w_kv_b, q_norm_g, k_norm_g, w_out,
           w_gate_up, w_down):
    n_batch, seq, d = x.shape
    ctx_len = ctx.shape[1]
    depth = w_in.shape[0]
    assert ctx_len == TM and seq % TM == 0 and seq % GRID_W == 0
    n_lat = n_batch * seq
    n_lat_tiles = n_lat // TM
    tiles_per_seq = seq // TM
    n_all_tiles = n_lat_tiles + n_batch * ctx_len // TM

    x_all = jnp.concatenate([x.reshape(n_lat, d), ctx.reshape(n_batch * ctx_len, d)], axis=0)
    n_rows = -(-(n_batch + 1) // SUBLANES) * SUBLANES
    cc = jnp.concatenate([c, c_ctx[None, :], jnp.zeros((n_rows - n_batch - 1, d), F32)], axis=0)
    mod = _mod_call(cc, w_ada, b_ada).reshape(depth, n_rows, N_MOD, d)
    cos_tab, sin_tab = _rope_tables(seq)

    geo = dict(n_lat_tiles=n_lat_tiles, tiles_per_seq=tiles_per_seq, n_batch=n_batch)
    for l in range(depth):
        wts = _layer_weights(l, norm1_g, norm2_g, w_in, sgu_norm_g, w_spatial, b_spatial, w_pool,
                             pool_scale, q_a_norm_g, w_q_b, kv_a_norm_g, w_kv_b, q_norm_g, k_norm_g,
                             w_out, w_gate_up, w_down)
        with_ctx = l < depth - 1
        n_tiles = n_all_tiles if with_ctx else n_lat_tiles
        ya, bpool, q, k, v = _pre_call(x_all, mod, l, wts, cos_tab, sin_tab, **geo)
        o = _attn_call(q, k, v, n_batch=n_batch, seq=seq, ctx_len=ctx_len, with_ctx=with_ctx)
        x_all = _post_call(x_all, mod, l, ya, bpool, o, wts, n_tiles=n_tiles, seq=seq,
                           ctx_len=ctx_len, **geo)
        x_all = _ffn_call(x_all, mod, l, wts, n_tiles=n_tiles, **geo)
    return x_all[:n_lat].reshape(n_batch, seq, d)
```

```python
import functools

import jax
import jax.numpy as jnp
import numpy as np
from jax import lax
from jax.experimental import pallas as pl
from jax.experimental.pallas import tpu as pltpu

F32 = jnp.float32
BF16 = jnp.bfloat16

GRID_W = 64
CHUNK = 128
A_HEADS = 4
POOL_WINDOWS = (2, 4, 8, 16)
V_DIM = 64
QK_NOPE = 64
QK_ROPE = 32
QK_HEAD = QK_NOPE + QK_ROPE
Q_RANK = 256
KV_RANK = 128
ROPE_BASE = 10000.0
EPS = 1e-6
N_MOD = 6

LANES = 128
SUBLANES = 8
HALO = 8
VMEM_LIMIT = 56 * 1024 * 1024

TM_PRE = 256
TM_POST = 256
TM_FFN = 512
TQ = 256
FF_CHUNK = 256
N_KV_BLOCKS = 2
HEADS_PER_GROUP = 2
PV_ROWS = 80


def _rms(x, g):
    return x * lax.rsqrt(jnp.mean(x * x, axis=-1, keepdims=True) + EPS) * g


def _params(*sem):
    return pltpu.CompilerParams(dimension_semantics=sem, vmem_limit_bytes=VMEM_LIMIT)


def _const_spec(shape):
    nd = len(shape)
    return pl.BlockSpec(shape, lambda *_: (0,) * nd, pipeline_mode=pl.Buffered(1))


def _mod_spec(layer, n_lat_tiles, tiles_per_seq, n_batch, d):
    return pl.BlockSpec(
        (None, None, N_MOD, d),
        lambda i: (layer, jnp.where(i < n_lat_tiles, i // tiles_per_seq, n_batch), 0, 0))


def _mod_kernel(c_ref, w_ref, b_ref, o_ref):
    c = c_ref[...]
    s = c * jax.nn.sigmoid(c)
    o_ref[...] = jnp.dot(s, w_ref[...], preferred_element_type=F32) + b_ref[...]


def _mod_call(cc, w_ada, b_ada):
    n_layer, d, n_out = w_ada.shape
    r = cc.shape[0]
    tn = 1024
    return pl.pallas_call(
        _mod_kernel,
        grid=(n_layer, n_out // tn),
        in_specs=[
            pl.BlockSpec((r, d), lambda l, j: (0, 0)),
            pl.BlockSpec((None, d, tn), lambda l, j: (l, 0, j)),
            pl.BlockSpec((None, 1, tn), lambda l, j: (l, 0, j)),
        ],
        out_specs=pl.BlockSpec((None, r, tn), lambda l, j: (l, 0, j)),
        out_shape=jax.ShapeDtypeStruct((n_layer, r, n_out), F32),
        compiler_params=_params("parallel", "parallel"),
        name="mod",
    )(cc, w_ada, b_ada.reshape(n_layer, 1, n_out))


def _swap_pairs(x):
    lane = lax.broadcasted_iota(jnp.int32, x.shape, 1)
    nxt = pltpu.roll(x, x.shape[1] - 1, 1)
    prv = pltpu.roll(x, 1, 1)
    return jnp.where(lane % 2 == 0, nxt, prv)


def _pre_kernel(x_ref, mod_ref, g1_ref, w_in_ref, sgu_g_ref, w_sp_ref, b_sp_ref,
                qa_g_ref, w_qb_ref, kva_g_ref, w_kb_ref, w_vbt_ref, qg_ref, kg_ref, vone_ref,
                cos_ref, sin_ref,
                ya_ref, b_ref, q_ref, k_ref, vt_ref, *, w_a, w_b, n_heads):
    x = x_ref[...]
    sh1 = mod_ref[0:1, :]
    sc1 = mod_ref[1:2, :]
    h = _rms(x, g1_ref[...]) * (1.0 + sc1) + sh1
    p = jnp.dot(h.astype(BF16), w_in_ref[...], preferred_element_type=F32)

    a = p[:, :2 * w_a]
    ga = 0.5 * a * (1.0 + lax.erf(a * np.float32(1.0 / np.sqrt(2.0))))
    u = ga[:, :w_a]
    vg = _rms(ga[:, w_a:], sgu_g_ref[...]).astype(BF16)
    head_of_lane = lax.broadcasted_iota(jnp.int32, (CHUNK, w_a), 1) // (w_a // A_HEADS)
    for c in range(x.shape[0] // CHUNK):
        rows = slice(c * CHUNK, (c + 1) * CHUNK)
        r = jnp.dot(w_sp_ref[...], vg[rows, :], preferred_element_type=F32)
        mixed = r[0:CHUNK, :]
        for hh in range(1, A_HEADS):
            mixed = jnp.where(head_of_lane == hh, r[hh * CHUNK:(hh + 1) * CHUNK, :], mixed)
        ya_ref[rows, :] = (u[rows, :] * (mixed + b_sp_ref[...])).astype(ya_ref.dtype)

    o = 2 * w_a
    b_ref[...] = p[:, o:o + w_b]

    o += w_b
    pqn = _rms(p[:, o:o + Q_RANK], qa_g_ref[...]).astype(BF16)
    o += Q_RANK
    pkvn = _rms(p[:, o:o + KV_RANK], kva_g_ref[...]).astype(BF16)
    o += KV_RANK
    kr = p[:, o:o + LANES]
    q_raw = jnp.dot(pqn, w_qb_ref[...], preferred_element_type=F32)
    k_raw = jnp.dot(pkvn, w_kb_ref[...], preferred_element_type=F32)
    cos = cos_ref[...]
    sin = sin_ref[...]
    for hh in range(n_heads):
        cols = slice(hh * LANES, (hh + 1) * LANES)
        qh = q_raw[:, cols]
        qn = qh * lax.rsqrt(jnp.sum(qh * qh, axis=-1, keepdims=True) * (1.0 / QK_HEAD) + EPS) * qg_ref[...]
        q_ref[hh] = (qn * cos + _swap_pairs(qn) * sin).astype(q_ref.dtype)
        kh = k_raw[:, cols] + kr
        kn = kh * lax.rsqrt(jnp.sum(kh * kh, axis=-1, keepdims=True) * (1.0 / QK_HEAD) + EPS) * kg_ref[...]
        k_ref[hh] = (kn * cos + _swap_pairs(kn) * sin).astype(k_ref.dtype)
    vt = lax.dot_general(w_vbt_ref[...], pkvn, (((1,), (1,)), ((), ())), preferred_element_type=F32)
    for hh in range(n_heads):
        vt_ref[hh] = (vt[hh * LANES:(hh + 1) * LANES, :] + vone_ref[...]).astype(vt_ref.dtype)


def _pre_call(x_all, mod, layer, wts, cos_tab, sin_tab, *, n_lat, seq, n_batch, ctx_len):
    nt, d = x_all.shape
    tm = TM_PRE
    n_lat_tiles = n_lat // tm
    tiles_per_seq = seq // tm
    w_a = wts["sgu_g"].shape[1]
    w_b = wts["pool_scale"].shape[1]
    n_heads = wts["w_qb"].shape[1] // LANES
    assert ctx_len == tm

    def ridx(i):
        return jnp.where(i < n_lat_tiles, i % tiles_per_seq, tiles_per_seq)

    tok = lambda w: pl.BlockSpec((tm, w), lambda i: (i, 0))
    consts = [wts[k] for k in ("g1", "w_in", "sgu_g", "w_sp", "b_sp", "qa_g", "w_qb", "kva_g",
                               "w_kb", "w_vbt", "qg", "kg", "vone")]
    in_specs = [tok(d), _mod_spec(layer, n_lat_tiles, tiles_per_seq, n_batch, d)]
    in_specs += [_const_spec(c.shape) for c in consts]
    in_specs += [pl.BlockSpec((tm, LANES), lambda i: (ridx(i), 0))] * 2
    out_shape = [jax.ShapeDtypeStruct((nt, w_a), BF16), jax.ShapeDtypeStruct((nt, w_b), F32),
                 jax.ShapeDtypeStruct((n_heads, nt, LANES), BF16),
                 jax.ShapeDtypeStruct((n_heads, nt, LANES), BF16),
                 jax.ShapeDtypeStruct((n_heads, LANES, nt), BF16)]
    head_tok = pl.BlockSpec((n_heads, tm, LANES), lambda i: (0, i, 0))

    def key_tile(i):
        lat = (i // tiles_per_seq) * (tiles_per_seq + 1) + 1 + i % tiles_per_seq
        return jnp.where(i < n_lat_tiles, lat, (i - n_lat_tiles) * (tiles_per_seq + 1))
    return pl.pallas_call(
        functools.partial(_pre_kernel, w_a=w_a, w_b=w_b, n_heads=n_heads),
        grid=(nt // tm,),
        in_specs=in_specs,
        out_specs=[tok(w_a), tok(w_b), head_tok,
                   pl.BlockSpec((n_heads, tm, LANES), lambda i: (0, key_tile(i), 0)),
                   pl.BlockSpec((n_heads, LANES, tm), lambda i: (0, 0, key_tile(i)))],
        out_shape=out_shape,
        compiler_params=_params("parallel"),
        name="pre",
    )(x_all, mod, *consts, cos_tab, sin_tab)


def _attend(q_ref, k_ref, vt_ref, ot_ref, s_ref, acc_ref, n_keys, n_heads):
    tq = q_ref.shape[1]
    grp = HEADS_PER_GROUP
    n_groups = n_heads // grp
    n_blocks = N_KV_BLOCKS if n_keys % (N_KV_BLOCKS * LANES) == 0 else 1
    blk = n_keys // n_blocks

    def stage(gi, st, ms, mxs):
        rows = pl.ds(st, blk)
        mxs = list(mxs)
        for g in range(grp):
            if ms is not None:
                hh = gi * grp + g
                p = jnp.exp2(s_ref[(gi % 2) * grp + g, rows, :] - ms[g]).astype(BF16)
                acc_ref[g] += jnp.dot(vt_ref[hh, 0:PV_ROWS, rows], p,
                                      preferred_element_type=F32)
            if gi + 1 < n_groups:
                hn = (gi + 1) * grp + g
                sn = lax.dot_general(k_ref[hn, rows, :], q_ref[hn], (((1,), (1,)), ((), ())),
                                     preferred_element_type=F32)
                s_ref[((gi + 1) % 2) * grp + g, rows, :] = sn
                mxs[g] = jnp.maximum(mxs[g], jnp.max(sn.reshape(blk // SUBLANES, SUBLANES, tq), axis=0))
        return tuple(mxs)

    def sweep(gi, ms):
        mxs = tuple(jnp.full((SUBLANES, tq), -jnp.inf, F32) for _ in range(grp))
        if n_blocks == 1:
            return stage(gi, 0, ms, mxs)
        return lax.fori_loop(0, n_blocks, lambda j, c: stage(gi, pl.multiple_of(j * blk, LANES), ms, c), mxs)

    mxs = sweep(-1, None)
    for gi in range(n_groups):
        acc_ref[...] = jnp.zeros(acc_ref.shape, F32)
        mxs = sweep(gi, [jnp.max(mx, axis=0, keepdims=True) for mx in mxs])
        for g in range(grp):
            out = acc_ref[g, 0:V_DIM, :] / acc_ref[g, V_DIM:V_DIM + 1, :]
            hh = gi * grp + g
            ot_ref[hh * V_DIM:(hh + 1) * V_DIM, :] = out.astype(ot_ref.dtype)


def _attn_kernel(q_ref, k_ref, vt_ref, ot_ref, s_ref, acc_ref, *, nq, ctx_len, n_heads, with_ctx):
    n_keys = k_ref.shape[1]
    if not with_ctx:
        _attend(q_ref, k_ref, vt_ref, ot_ref, s_ref, acc_ref, n_keys, n_heads)
        return
    i = pl.program_id(1)

    @pl.when(i < nq)
    def _():
        _attend(q_ref, k_ref, vt_ref, ot_ref, s_ref, acc_ref, n_keys, n_heads)

    @pl.when(i == nq)
    def _():
        _attend(q_ref, k_ref, vt_ref, ot_ref, s_ref, acc_ref, ctx_len, n_heads)


def _attn_call(q, k, vt, *, n_batch, seq, ctx_len, with_ctx):
    n_heads, nt, _ = q.shape
    assert ctx_len == TQ and n_heads % HEADS_PER_GROUP == 0
    nq = seq // TQ
    n_lat = n_batch * seq
    n_keys = seq + ctx_len

    def q_tile(b, i):
        return jnp.where(i < nq, b * nq + i, n_lat // TQ + b)

    one = pl.Buffered(1)
    return pl.pallas_call(
        functools.partial(_attn_kernel, nq=nq, ctx_len=ctx_len, n_heads=n_heads, with_ctx=with_ctx),
        grid=(n_batch, nq + (1 if with_ctx else 0)),
        in_specs=[pl.BlockSpec((n_heads, TQ, LANES), lambda b, i: (0, q_tile(b, i), 0)),
                  pl.BlockSpec((n_heads, n_keys, LANES), lambda b, i: (0, b, 0), pipeline_mode=one),
                  pl.BlockSpec((n_heads, LANES, n_keys), lambda b, i: (0, 0, b), pipeline_mode=one)],
        out_specs=pl.BlockSpec((n_heads * V_DIM, TQ), lambda b, i: (0, q_tile(b, i))),
        out_shape=jax.ShapeDtypeStruct((n_heads * V_DIM, nt if with_ctx else n_lat), BF16),
        scratch_shapes=[pltpu.VMEM((2 * HEADS_PER_GROUP, n_keys, TQ), F32),
                        pltpu.VMEM((HEADS_PER_GROUP, PV_ROWS, TQ), F32)],
        compiler_params=_params("parallel", "arbitrary"),
        name="attn",
    )(q, k, vt)


def _post_kernel(x_ref, mod_ref, ya_ref, b_ref, bp_ref, bn_ref, ot_ref, w_out_ref, w_pool_ref,
                 ps_ref, xo_ref, ext_ref, *, n_lat_tiles, tiles_per_seq, seq, ctx_len):
    i = pl.program_id(0)
    tm, w_b = b_ref.shape
    is_lat = i < n_lat_tiles
    t_in_seq = jnp.where(is_lat, i % tiles_per_seq, 0)
    n_seq = jnp.where(is_lat, seq, ctx_len)
    first = t_in_seq == 0
    last = jnp.where(is_lat, t_in_seq == tiles_per_seq - 1, True)

    b = b_ref[...]
    zero_halo = jnp.zeros((HALO, w_b), F32)
    ext_ref[0:HALO, :] = jnp.where(first, zero_halo, bp_ref[...])
    ext_ref[HALO:HALO + tm, :] = b
    ext_ref[HALO + tm:, :] = jnp.where(last, zero_halo, bn_ref[...])

    def shifted(off):
        return ext_ref[HALO + off:HALO + off + tm, :]

    pos = t_in_seq * tm + lax.broadcasted_iota(jnp.int32, (tm, w_b), 0)
    group = lax.broadcasted_iota(jnp.int32, (tm, w_b), 1) // (w_b // len(POOL_WINDOWS))
    acc = None
    win = None
    cnt = None
    done = 0
    for g, w in enumerate(POOL_WINDOWS):
        half = w // 2
        for off in list(range(-half, -done)) + list(range(done, half)):
            acc = shifted(off) if acc is None else acc + shifted(off)
        done = half
        c_w = jnp.minimum(pos + half, n_seq) - jnp.maximum(pos - half, 0)
        win = acc if win is None else jnp.where(group == g, acc, win)
        cnt = c_w if cnt is None else jnp.where(group == g, c_w, cnt)
    d = win / cnt.astype(F32) - b
    yb = jnp.dot(d.astype(BF16), w_pool_ref[...], preferred_element_type=F32) * ps_ref[...]

    w_ab = ya_ref.shape[1] + w_b
    y_ab = jnp.concatenate([ya_ref[...], yb.astype(BF16)], axis=-1)
    mix = jnp.dot(y_ab, w_out_ref[0:w_ab, :], preferred_element_type=F32)
    mix += lax.dot_general(ot_ref[...], w_out_ref[w_ab:, :], (((0,), (0,)), ((), ())),
                           preferred_element_type=F32)
    g1 = mod_ref[2:3, :]
    xo_ref[...] = x_ref[...] + g1 * mix


def _post_call(x_all, mod, layer, ya, bpool, o, wts, *, n_tok, n_lat, n_batch, seq, ctx_len):
    nt, d = x_all.shape
    tm = TM_POST
    assert ctx_len == tm
    n_lat_tiles = n_lat // tm
    tiles_per_seq = seq // tm
    w_a = ya.shape[1]
    w_b = bpool.shape[1]
    w_c = o.shape[0]
    hb = tm // HALO
    last_halo = nt // HALO - 1

    tok = lambda w: pl.BlockSpec((tm, w), lambda i: (i, 0))
    return pl.pallas_call(
        functools.partial(_post_kernel, n_lat_tiles=n_lat_tiles, tiles_per_seq=tiles_per_seq,
                          seq=seq, ctx_len=ctx_len),
        grid=(n_tok // tm,),
        in_specs=[tok(d), _mod_spec(layer,n_lat_tiles, tiles_per_seq, n_batch, d),
                  tok(w_a), tok(w_b),
                  pl.BlockSpec((HALO, w_b), lambda i: (jnp.maximum(i * hb - 1, 0), 0)),
                  pl.BlockSpec((HALO, w_b), lambda i: (jnp.minimum((i + 1) * hb, last_halo), 0)),
                  pl.BlockSpec((w_c, tm), lambda i: (0, i)),
                  _const_spec(wts["w_out"].shape), _const_spec(wts["w_pool"].shape),
                  _const_spec(wts["pool_scale"].shape)],
        out_specs=tok(d),
        out_shape=jax.ShapeDtypeStruct((n_tok, d), F32),
        scratch_shapes=[pltpu.VMEM((tm + 2 * HALO, w_b), F32)],
        compiler_params=_params("parallel"),
        name="post",
    )(x_all, mod, ya, bpool, bpool, bpool, o, wts["w_out"], wts["w_pool"], wts["pool_scale"])


def _ffn_kernel(x_ref, mod_ref, g2_ref, w_gu_ref, w_dn_ref, xo_ref, *, d_ff):
    x = x_ref[...]
    sh2 = mod_ref[3:4, :]
    sc2 = mod_ref[4:5, :]
    gate2 = mod_ref[5:6, :]
    h = (_rms(x, g2_ref[...]) * (1.0 + sc2) + sh2).astype(BF16)
    fc = FF_CHUNK
    y = None
    for j in range(d_ff // fc):
        gt = jnp.dot(h, w_gu_ref[:, j * fc:(j + 1) * fc], preferred_element_type=F32)
        up = jnp.dot(h, w_gu_ref[:, d_ff + j * fc:d_ff + (j + 1) * fc], preferred_element_type=F32)
        t = (gt * jax.nn.sigmoid(gt) * up).astype(BF16)
        part = jnp.dot(t, w_dn_ref[j * fc:(j + 1) * fc, :], preferred_element_type=F32)
        y = part if y is None else y + part
    xo_ref[...] = x + gate2 * y


def _ffn_call(x_all, mod, layer, wts, *, n_tok, n_lat, n_batch, seq):
    nt, d = x_all.shape
    tm = TM_FFN
    d_ff = wts["w_dn"].shape[0]
    assert d_ff % FF_CHUNK == 0 and n_tok % tm == 0
    tok = pl.BlockSpec((tm, d), lambda i: (i, 0))
    return pl.pallas_call(
        functools.partial(_ffn_kernel, d_ff=d_ff),
        grid=(n_tok // tm,),
        in_specs=[tok, _mod_spec(layer,n_lat // tm, seq // tm, n_batch, d),
                  _const_spec(wts["g2"].shape), _const_spec(wts["w_gu"].shape),
                  _const_spec(wts["w_dn"].shape)],
        out_specs=tok,
        out_shape=jax.ShapeDtypeStruct((n_tok, d), F32),
        compiler_params=_params("parallel"),
        name="ffn",
    )(x_all, mod, wts["g2"], wts["w_gu"], wts["w_dn"])


def _rope_tables(seq):
    rows = seq // GRID_W
    row = jnp.repeat(jnp.arange(rows), GRID_W).astype(F32)
    col = jnp.tile(jnp.arange(GRID_W), rows).astype(F32)
    n_freq = QK_ROPE // 4
    inv = ROPE_BASE ** (-jnp.arange(n_freq, dtype=F32) / n_freq)
    ang = jnp.concatenate([row[:, None] * inv, col[:, None] * inv], axis=-1)
    cos = jnp.repeat(jnp.cos(ang), 2, axis=-1)
    sin = jnp.repeat(jnp.sin(ang), 2, axis=-1) * jnp.tile(jnp.array([-1.0, 1.0], F32), QK_ROPE // 2)
    pad = LANES - QK_HEAD
    cos_t = jnp.concatenate([jnp.ones((seq, QK_NOPE), F32), cos, jnp.zeros((seq, pad), F32)], axis=-1)
    sin_t = jnp.concatenate([jnp.zeros((seq, QK_NOPE), F32), sin, jnp.zeros((seq, pad), F32)], axis=-1)
    ident_c = jnp.concatenate([jnp.ones((TM_PRE, QK_HEAD), F32), jnp.zeros((TM_PRE, pad), F32)], axis=-1)
    ident_s = jnp.zeros((TM_PRE, LANES), F32)
    return jnp.concatenate([cos_t, ident_c], axis=0), jnp.concatenate([sin_t, ident_s], axis=0)


def _layer_weights(l, norm1_g, norm2_g, w_in, sgu_norm_g, w_spatial, b_spatial, w_pool, pool_scale,
                   q_a_norm_g, w_q_b, kv_a_norm_g, w_kv_b, q_norm_g, k_norm_g, w_out, w_gate_up,
                   w_down):
    d = w_in.shape[1]
    w_a = sgu_norm_g.shape[1]
    w_b = pool_scale.shape[1]
    n_heads = w_q_b.shape[2] // QK_HEAD
    pad = LANES - QK_HEAD
    split = 2 * w_a + w_b + Q_RANK + KV_RANK
    wi = w_in[l]
    w_in_p = jnp.concatenate([wi[:, :split], jnp.zeros((d, QK_NOPE), F32), wi[:, split:],
                              jnp.zeros((d, pad), F32)], axis=1)
    w_qb = jnp.pad(w_q_b[l].reshape(Q_RANK, n_heads, QK_HEAD), ((0, 0), (0, 0), (0, pad)))
    kvb = w_kv_b[l].reshape(KV_RANK, n_heads, QK_NOPE + V_DIM)
    w_kb = jnp.pad(kvb[:, :, :QK_NOPE], ((0, 0), (0, 0), (0, LANES - QK_NOPE)))
    w_vb = jnp.pad(kvb[:, :, QK_NOPE:], ((0, 0), (0, 0), (0, LANES - V_DIM)))
    vone = jnp.zeros((LANES, TM_PRE), F32).at[V_DIM, :].set(1.0)
    return dict(
        g1=norm1_g[l][None, :], g2=norm2_g[l][None, :],
        w_in=w_in_p.astype(BF16),
        sgu_g=sgu_norm_g[l][None, :],
        w_sp=w_spatial[l].reshape(A_HEADS * CHUNK, CHUNK).astype(BF16),
        b_sp=jnp.repeat(b_spatial[l].T, w_a // A_HEADS, axis=1),
        qa_g=q_a_norm_g[l][None, :], kva_g=kv_a_norm_g[l][None, :],
        w_qb=w_qb.reshape(Q_RANK, -1).astype(BF16),
        w_kb=w_kb.reshape(KV_RANK, -1).astype(BF16),
        w_vbt=w_vb.reshape(KV_RANK, -1).T.astype(BF16),
        qg=jnp.pad(q_norm_g[l] * np.float32(QK_HEAD ** -0.5 * np.log2(np.e)), (0, pad))[None, :],
        kg=jnp.pad(k_norm_g[l], (0, pad))[None, :],
        vone=vone,
        w_out=w_out[l].astype(BF16),
        w_pool=jax.scipy.linalg.block_diag(*[w_pool[l, g] for g in range(w_pool.shape[1])]).astype(BF16),
        pool_scale=pool_scale[l][None, :],
        w_gu=w_gate_up[l].astype(BF16), w_dn=w_down[l].astype(BF16),
    )


def kernel(x, c, ctx, c_ctx, norm1_g, norm2_g, w_ada, b_ada, w_in, sgu_norm_g, w_spatial, b_spatial,
           w_pool, pool_scale, q_a_norm_g, w_q_b, kv_a_norm_g, w_kv_b, q_norm_g, k_norm_g, w_out,
           w_gate_up, w_down):
    n_batch, seq, d = x.shape
    ctx_len = ctx.shape[1]
    depth = w_in.shape[0]
    n_lat = n_batch * seq
    n_ctx = n_batch * ctx_len
    assert seq % GRID_W == 0
    for tm in (TM_PRE, TM_POST, TM_FFN):
        assert seq % tm == 0 and n_ctx % tm == 0

    x_all = jnp.concatenate([x.reshape(n_lat, d), ctx.reshape(n_ctx, d)], axis=0)
    n_rows = -(-(n_batch + 1) // SUBLANES) * SUBLANES
    cc = jnp.concatenate([c, c_ctx[None, :], jnp.zeros((n_rows - n_batch - 1, d), F32)], axis=0)
    mod = _mod_call(cc, w_ada, b_ada).reshape(depth, n_rows, N_MOD, d)
    cos_tab, sin_tab = _rope_tables(seq)

    for l in range(depth):
        wts = _layer_weights(l, norm1_g, norm2_g, w_in, sgu_norm_g, w_spatial, b_spatial, w_pool,
                             pool_scale, q_a_norm_g, w_q_b, kv_a_norm_g, w_kv_b, q_norm_g, k_norm_g,
                             w_out, w_gate_up, w_down)
        with_ctx = l < depth - 1
        n_tok = n_lat + n_ctx if with_ctx else n_lat
        ya, bpool, q, k, vt = _pre_call(x_all, mod, l, wts, cos_tab, sin_tab, n_lat=n_lat, seq=seq,
                                       n_batch=n_batch, ctx_len=ctx_len)
        o = _attn_call(q, k, vt, n_batch=n_batch, seq=seq, ctx_len=ctx_len, with_ctx=with_ctx)
        x_all = _post_call(x_all, mod, l, ya, bpool, o, wts, n_tok=n_tok, n_lat=n_lat,
                           n_batch=n_batch, seq=seq, ctx_len=ctx_len)
        x_all = _ffn_call(x_all, mod, l, wts, n_tok=n_tok, n_lat=n_lat, n_batch=n_batch, seq=seq)
    return x_all[:n_lat].reshape(n_batch, seq, d)
```

```python
import functools

import jax
import jax.numpy as jnp
import numpy as np
from jax import lax
from jax.experimental import pallas as pl
from jax.experimental.pallas import tpu as pltpu

F32 = jnp.float32
BF16 = jnp.bfloat16

GRID_W = 64
CHUNK = 128
A_HEADS = 4
POOL_WINDOWS = (2, 4, 8, 16)
V_DIM = 64
QK_NOPE = 64
QK_ROPE = 32
QK_HEAD = QK_NOPE + QK_ROPE
Q_RANK = 256
KV_RANK = 128
ROPE_BASE = 10000.0
EPS = 1e-6
N_MOD = 6

LANES = 128
SUBLANES = 8
HALO = 8
VMEM_LIMIT = 56 * 1024 * 1024

TM_PRE = 256
TM_TAIL = 512
TQ = 256
FF_CHUNK = 256
N_KV_BLOCKS = 2
HEADS_PER_GROUP = 2
PV_ROWS = 80


def _rms(x, g):
    return x * lax.rsqrt(jnp.mean(x * x, axis=-1, keepdims=True) + EPS) * g


def _params(*sem):
    return pltpu.CompilerParams(dimension_semantics=sem, vmem_limit_bytes=VMEM_LIMIT)


def _const_spec(shape):
    nd = len(shape)
    return pl.BlockSpec(shape, lambda *_: (0,) * nd, pipeline_mode=pl.Buffered(1))


def _token_specs(x_parts, tm, n_lat_tiles):
    (xa, row_a), (xb, row_b) = x_parts
    assert row_a % tm == 0 and row_b % tm == 0
    d = xa.shape[1]
    return [pl.BlockSpec((tm, d), lambda i: (row_a // tm + jnp.minimum(i, n_lat_tiles - 1), 0)),
            pl.BlockSpec((tm, d), lambda i: (row_b // tm + jnp.maximum(i - n_lat_tiles, 0), 0))]


def _load_tokens(xa_ref, xb_ref, n_lat_tiles):
    return jnp.where(pl.program_id(0) < n_lat_tiles, xa_ref[...], xb_ref[...])


def _mod_spec(layer, n_lat_tiles, tiles_per_seq, n_batch, d):
    return pl.BlockSpec(
        (None, None, N_MOD, d),
        lambda i: (layer, jnp.where(i < n_lat_tiles, i // tiles_per_seq, n_batch), 0, 0))


def _mod_kernel(c_ref, w_ref, b_ref, o_ref):
    c = c_ref[...]
    s = c * jax.nn.sigmoid(c)
    o_ref[...] = jnp.dot(s, w_ref[...], preferred_element_type=F32) + b_ref[...]


def _mod_call(cc, w_ada, b_ada):
    n_layer, d, n_out = w_ada.shape
    r = cc.shape[0]
    tn = 1024
    return pl.pallas_call(
        _mod_kernel,
        grid=(n_layer, n_out // tn),
        in_specs=[
            pl.BlockSpec((r, d), lambda l, j: (0, 0)),
            pl.BlockSpec((None, d, tn), lambda l, j: (l, 0, j)),
            pl.BlockSpec((None, 1, tn), lambda l, j: (l, 0, j)),
        ],
        out_specs=pl.BlockSpec((None, r, tn), lambda l, j: (l, 0, j)),
        out_shape=jax.ShapeDtypeStruct((n_layer, r, n_out), F32),
        compiler_params=_params("parallel", "parallel"),
        name="mod",
    )(cc, w_ada, b_ada.reshape(n_layer, 1, n_out))


def _swap_pairs(x):
    lane = lax.broadcasted_iota(jnp.int32, x.shape, 1)
    nxt = pltpu.roll(x, x.shape[1] - 1, 1)
    prv = pltpu.roll(x, 1, 1)
    return jnp.where(lane % 2 == 0, nxt, prv)


def _pre_kernel(xa_ref, xb_ref, mod_ref, g1_ref, w_in_ref, sgu_g_ref, w_sp_ref, b_sp_ref,
                qa_g_ref, w_qb_ref, kva_g_ref, w_kb_ref, w_vbt_ref, qg_ref, kg_ref, vone_ref,
                cos_ref, sin_ref,
                ya_ref, b_ref, q_ref, k_ref, vt_ref, *, w_a, w_b, n_heads, n_lat_tiles):
    x = _load_tokens(xa_ref, xb_ref, n_lat_tiles)
    sh1 = mod_ref[0:1, :]
    sc1 = mod_ref[1:2, :]
    h = _rms(x, g1_ref[...]) * (1.0 + sc1) + sh1
    p = jnp.dot(h.astype(BF16), w_in_ref[...], preferred_element_type=F32)

    a = p[:, :2 * w_a]
    ga = 0.5 * a * (1.0 + lax.erf(a * np.float32(1.0 / np.sqrt(2.0))))
    u = ga[:, :w_a]
    vg = _rms(ga[:, w_a:], sgu_g_ref[...]).astype(BF16)
    head_of_lane = lax.broadcasted_iota(jnp.int32, (CHUNK, w_a), 1) // (w_a // A_HEADS)
    for c in range(x.shape[0] // CHUNK):
        rows = slice(c * CHUNK, (c + 1) * CHUNK)
        r = jnp.dot(w_sp_ref[...], vg[rows, :], preferred_element_type=F32)
        mixed = r[0:CHUNK, :]
        for hh in range(1, A_HEADS):
            mixed = jnp.where(head_of_lane == hh, r[hh * CHUNK:(hh + 1) * CHUNK, :], mixed)
        ya_ref[rows, :] = (u[rows, :] * (mixed + b_sp_ref[...])).astype(ya_ref.dtype)

    o = 2 * w_a
    b_ref[...] = p[:, o:o + w_b]

    o += w_b
    pqn = _rms(p[:, o:o + Q_RANK], qa_g_ref[...]).astype(BF16)
    o += Q_RANK
    pkvn = _rms(p[:, o:o + KV_RANK], kva_g_ref[...]).astype(BF16)
    o += KV_RANK
    kr = p[:, o:o + LANES]
    hw = n_heads * LANES
    q2 = jnp.dot(pqn, w_qb_ref[...], preferred_element_type=F32)
    k_raw = jnp.dot(pkvn, w_kb_ref[...], preferred_element_type=F32)
    cos = cos_ref[...]
    sin = sin_ref[...]
    gcos = cos * qg_ref[...]
    krg = kr * kg_ref[...]
    kr_rot = krg * cos + _swap_pairs(krg) * sin
    ss_kr = jnp.sum(kr * kr, axis=-1, keepdims=True)
    for hh in range(n_heads):
        cols = slice(hh * LANES, (hh + 1) * LANES)
        qh = q2[:, cols]
        rq = lax.rsqrt(jnp.sum(qh * qh, axis=-1, keepdims=True) * (1.0 / QK_HEAD) + EPS)
        q_ref[hh] = (rq * (qh * gcos + q2[:, hw + hh * LANES:hw + (hh + 1) * LANES] * sin)).astype(q_ref.dtype)
        kh = k_raw[:, cols]
        rk = lax.rsqrt((jnp.sum(kh * kh, axis=-1, keepdims=True) + ss_kr) * (1.0 / QK_HEAD) + EPS)
        k_ref[hh] = (rk * (kh * kg_ref[...] + kr_rot)).astype(k_ref.dtype)
    vt = lax.dot_general(w_vbt_ref[...], pkvn, (((1,), (1,)), ((), ())), preferred_element_type=F32)
    for hh in range(n_heads):
        vt_ref[hh] = (vt[hh * LANES:(hh + 1) * LANES, :] + vone_ref[...]).astype(vt_ref.dtype)


def _pre_call(x_parts, mod, layer, wts, cos_tab, sin_tab, *, n_lat, n_ctx, seq, n_batch, ctx_len):
    nt, d = n_lat + n_ctx, x_parts[0][0].shape[1]
    tm = TM_PRE
    n_lat_tiles = n_lat // tm
    tiles_per_seq = seq // tm
    w_a = wts["sgu_g"].shape[1]
    w_b = wts["pool_scale"].shape[1]
    n_heads = wts["w_kb"].shape[1] // LANES
    assert ctx_len == tm

    def ridx(i):
        return jnp.where(i < n_lat_tiles, i % tiles_per_seq, tiles_per_seq)

    tok = lambda w: pl.BlockSpec((tm, w), lambda i: (i, 0))
    consts = [wts[k] for k in ("g1", "w_in", "sgu_g", "w_sp", "b_sp", "qa_g", "w_qb", "kva_g",
                               "w_kb", "w_vbt", "qg", "kg", "vone")]
    in_specs = _token_specs(x_parts, tm, n_lat_tiles)
    in_specs += [_mod_spec(layer, n_lat_tiles, tiles_per_seq, n_batch, d)]
    in_specs += [_const_spec(c.shape) for c in consts]
    in_specs += [pl.BlockSpec((tm, LANES), lambda i: (ridx(i), 0))] * 2
    out_shape = [jax.ShapeDtypeStruct((nt, w_a), BF16), jax.ShapeDtypeStruct((nt, w_b), F32),
                 jax.ShapeDtypeStruct((n_heads, nt, LANES), BF16),
                 jax.ShapeDtypeStruct((n_heads, nt, LANES), BF16),
                 jax.ShapeDtypeStruct((n_heads, LANES, nt), BF16)]
    head_tok = pl.BlockSpec((n_heads, tm, LANES), lambda i: (0, i, 0))

    def key_tile(i):
        lat = (i // tiles_per_seq) * (tiles_per_seq + 1) + 1 + i % tiles_per_seq
        return jnp.where(i < n_lat_tiles, lat, (i - n_lat_tiles) * (tiles_per_seq + 1))
    return pl.pallas_call(
        functools.partial(_pre_kernel, w_a=w_a, w_b=w_b, n_heads=n_heads, n_lat_tiles=n_lat_tiles),
        grid=(nt // tm,),
        in_specs=in_specs,
        out_specs=[tok(w_a), tok(w_b), head_tok,
                   pl.BlockSpec((n_heads, tm, LANES), lambda i: (0, key_tile(i), 0)),
                   pl.BlockSpec((n_heads, LANES, tm), lambda i: (0, 0, key_tile(i)))],
        out_shape=out_shape,
        compiler_params=_params("parallel"),
        name="pre",
    )(x_parts[0][0], x_parts[1][0], mod, *consts, cos_tab, sin_tab)


def _when(cond, fn):
    if isinstance(cond, bool):
        if cond:
            fn()
    else:
        pl.when(cond)(fn)


def _attend(q_ref, qn_ref, k_ref, vt_ref, ot_ref, s_ref, acc_ref, mx_ref, n_keys, n_heads,
            is_first, has_next):
    tq = q_ref.shape[1]
    grp = HEADS_PER_GROUP
    n_groups = n_heads // grp
    assert n_groups % 2 == 0
    n_blocks = N_KV_BLOCKS if n_keys % (N_KV_BLOCKS * LANES) == 0 else 1
    blk = n_keys // n_blocks

    def stage(g_exp, g_qk, qsrc_ref, st, ms, mxs):
        rows = pl.ds(st, blk)
        mxs = list(mxs)
        for g in range(grp):
            if g_exp is not None:
                hh = g_exp * grp + g
                p = jnp.exp2(s_ref[(g_exp % 2) * grp + g, rows, :] - ms[g]).astype(BF16)
                acc_ref[g] += jnp.dot(vt_ref[hh, 0:PV_ROWS, rows], p,
                                      preferred_element_type=F32)
            if g_qk is not None:
                hn = g_qk * grp + g
                sn = lax.dot_general(k_ref[hn, rows, :], qsrc_ref[hn], (((1,), (1,)), ((), ())),
                                     preferred_element_type=F32)
                s_ref[(g_qk % 2) * grp + g, rows, :] = sn
                mxs[g] = jnp.maximum(mxs[g], jnp.max(sn.reshape(blk // SUBLANES, SUBLANES, tq), axis=0))
        return tuple(mxs)

    def sweep(g_exp, g_qk, qsrc_ref):
        ms = None
        if g_exp is not None:
            ms = [jnp.max(mx_ref[g], axis=0, keepdims=True) for g in range(grp)]
            acc_ref[...] = jnp.zeros(acc_ref.shape, F32)
        mxs = tuple(jnp.full((SUBLANES, tq), -jnp.inf, F32) for _ in range(grp))
        if n_blocks == 1:
            mxs = stage(g_exp, g_qk, qsrc_ref, 0, ms, mxs)
        else:
            mxs = lax.fori_loop(
                0, n_blocks,
                lambda j, c: stage(g_exp, g_qk, qsrc_ref, pl.multiple_of(j * blk, LANES), ms, c), mxs)
        if g_qk is not None:
            for g in range(grp):
                mx_ref[g] = mxs[g]

    def write_out(gi):
        for g in range(grp):
            out = acc_ref[g, 0:V_DIM, :] / acc_ref[g, V_DIM:V_DIM + 1, :]
            hh = gi * grp + g
            ot_ref[hh * V_DIM:(hh + 1) * V_DIM, :] = out.astype(ot_ref.dtype)

    _when(is_first, lambda: sweep(None, 0, q_ref))
    for gi in range(n_groups - 1):
        sweep(gi, gi + 1, q_ref)
        write_out(gi)
    last = n_groups - 1
    _when(has_next, lambda: sweep(last, 0, qn_ref))
    _when(jnp.logical_not(has_next) if not isinstance(has_next, bool) else not has_next,
          lambda: sweep(last, None, None))
    write_out(last)


def _attn_kernel(q_ref, qn_ref, k_ref, vt_ref, ot_ref, s_ref, acc_ref, mx_ref, *, nq, ctx_len, n_heads,
                 with_ctx):
    n_keys = k_ref.shape[1]
    i = pl.program_id(1)
    args = (q_ref, qn_ref, k_ref, vt_ref, ot_ref, s_ref, acc_ref, mx_ref)
    if not with_ctx:
        _attend(*args, n_keys, n_heads, i == 0, i < nq - 1)
        return

    @pl.when(i < nq)
    def _():
        _attend(*args, n_keys, n_heads, i == 0, i < nq - 1)

    @pl.when(i == nq)
    def _():
        _attend(*args, ctx_len, n_heads, True, False)


def _attn_call(q, k, vt, *, n_batch, seq, ctx_len, with_ctx):
    n_heads, nt, _ = q.shape
    assert ctx_len == TQ and n_heads % HEADS_PER_GROUP == 0
    nq = seq // TQ
    n_lat = n_batch * seq
    n_keys = seq + ctx_len

    def q_tile(b, i):
        return jnp.where(i < nq, b * nq + i, n_lat // TQ + b)

    one = pl.Buffered(1)
    return pl.pallas_call(
        functools.partial(_attn_kernel, nq=nq, ctx_len=ctx_len, n_heads=n_heads, with_ctx=with_ctx),
        grid=(n_batch, nq + (1 if with_ctx else 0)),
        in_specs=[pl.BlockSpec((n_heads, TQ, LANES), lambda b, i: (0, q_tile(b, i), 0)),
                  pl.BlockSpec((n_heads, TQ, LANES), lambda b, i: (0, b * nq + jnp.minimum(i + 1, nq - 1), 0)),
                  pl.BlockSpec((n_heads, n_keys, LANES), lambda b, i: (0, b, 0), pipeline_mode=one),
                  pl.BlockSpec((n_heads, LANES, n_keys), lambda b, i: (0, 0, b), pipeline_mode=one)],
        out_specs=pl.BlockSpec((n_heads * V_DIM, TQ), lambda b, i: (0, q_tile(b, i))),
        out_shape=jax.ShapeDtypeStruct((n_heads * V_DIM, nt if with_ctx else n_lat), BF16),
        scratch_shapes=[pltpu.VMEM((2 * HEADS_PER_GROUP, n_keys, TQ), F32),
                        pltpu.VMEM((HEADS_PER_GROUP, PV_ROWS, TQ), F32),
                        pltpu.VMEM((HEADS_PER_GROUP, SUBLANES, TQ), F32)],
        compiler_params=_params("arbitrary", "arbitrary"),
        name="attn",
    )(q, q, k, vt)


def _tail_kernel(xa_ref, xb_ref, mod_ref, ya_ref, b_ref, bp_ref, bn_ref, ot_ref, w_out_ref, w_pool_ref,
                 ps_ref, g2_ref, w_gu_ref, w_dn_ref, xo_ref, ext_ref, d_ref, *, n_lat_tiles, tiles_per_seq,
                 seq, ctx_len, d_ff):
    i = pl.program_id(0)
    tm, w_b = b_ref.shape
    row = lax.broadcasted_iota(jnp.int32, (tm, 1), 0)
    group = lax.broadcasted_iota(jnp.int32, (tm, w_b), 1) // (w_b // len(POOL_WINDOWS))

    def pooled_minus_identity(pos, n_seq, keep_prev, keep_next, mask_rows):
        zero_halo = jnp.zeros((HALO, w_b), F32)
        ext_ref[0:HALO, :] = jnp.where(keep_prev, bp_ref[...], zero_halo)
        ext_ref[HALO:HALO + tm, :] = b_ref[...]
        ext_ref[HALO + tm:, :] = jnp.where(keep_next, bn_ref[...], zero_halo)

        def shifted(off):
            rows = ext_ref[HALO + off:HALO + off + tm, :]
            if not mask_rows:
                return rows
            return jnp.where(jnp.logical_and(pos + off >= 0, pos + off < n_seq), rows, 0.0)

        acc = win = cnt = None
        done = 0
        for g, w in enumerate(POOL_WINDOWS):
            half = w // 2
            for off in list(range(-half, -done)) + list(range(done, half)):
                acc = shifted(off) if acc is None else acc + shifted(off)
            done = half
            c_w = jnp.minimum(pos + half, n_seq) - jnp.maximum(pos - half, 0)
            win = acc if win is None else jnp.where(group == g, acc, win)
            cnt = c_w if cnt is None else jnp.where(group == g, c_w, cnt)
        return win / cnt.astype(F32) - b_ref[...]

    t_in_seq = i % tiles_per_seq

    @pl.when(i < n_lat_tiles)
    def _():
        d_ref[...] = pooled_minus_identity(t_in_seq * tm + row, seq, t_in_seq > 0,
                                           t_in_seq < tiles_per_seq - 1, False)

    @pl.when(i >= n_lat_tiles)
    def _():
        pos = row
        for s in range(1, tm // ctx_len):
            pos = jnp.where(row >= s * ctx_len, row - s * ctx_len, pos)
        d_ref[...] = pooled_minus_identity(pos, ctx_len, False, False, True)

    yb = jnp.dot(d_ref[...].astype(BF16), w_pool_ref[...], preferred_element_type=F32) * ps_ref[...]

    w_ab = ya_ref.shape[1] + w_b
    y_ab = jnp.concatenate([ya_ref[...], yb.astype(BF16)], axis=-1)
    mix = jnp.dot(y_ab, w_out_ref[0:w_ab, :], preferred_element_type=F32)
    mix += lax.dot_general(ot_ref[...], w_out_ref[w_ab:, :], (((0,), (0,)), ((), ())),
                           preferred_element_type=F32)
    x = _load_tokens(xa_ref, xb_ref, n_lat_tiles) + mod_ref[2:3, :] * mix

    sh2 = mod_ref[3:4, :]
    sc2 = mod_ref[4:5, :]
    gate2 = mod_ref[5:6, :]
    h = (_rms(x, g2_ref[...]) * (1.0 + sc2) + sh2).astype(BF16)
    fc = FF_CHUNK
    y = None
    for j in range(d_ff // fc):
        gt = jnp.dot(h, w_gu_ref[:, j * fc:(j + 1) * fc], preferred_element_type=F32)
        up = jnp.dot(h, w_gu_ref[:, d_ff + j * fc:d_ff + (j + 1) * fc], preferred_element_type=F32)
        t = (gt * jax.nn.sigmoid(gt) * up).astype(BF16)
        part = jnp.dot(t, w_dn_ref[j * fc:(j + 1) * fc, :], preferred_element_type=F32)
        y = part if y is None else y + part
    xo_ref[...] = x + gate2 * y


def _tail_call(x_parts, mod, layer, ya, bpool, o, wts, *, n_tok, n_lat, n_batch, seq, ctx_len):
    nt, d = bpool.shape[0], x_parts[0][0].shape[1]
    tm = TM_TAIL
    assert tm % ctx_len == 0 and n_tok % tm == 0
    n_lat_tiles = n_lat // tm
    tiles_per_seq = seq // tm
    w_a = ya.shape[1]
    w_b = bpool.shape[1]
    w_c = o.shape[0]
    d_ff = wts["w_dn"].shape[0]
    assert d_ff % FF_CHUNK == 0
    hb = tm // HALO
    last_halo = nt // HALO - 1

    tok = lambda w: pl.BlockSpec((tm, w), lambda i: (i, 0))
    consts = [wts[k] for k in ("w_out", "w_pool", "pool_scale", "g2", "w_gu", "w_dn")]
    return pl.pallas_call(
        functools.partial(_tail_kernel, n_lat_tiles=n_lat_tiles, tiles_per_seq=tiles_per_seq,
                          seq=seq, ctx_len=ctx_len, d_ff=d_ff),
        grid=(n_tok // tm,),
        in_specs=_token_specs(x_parts, tm, n_lat_tiles) + [
                  _mod_spec(layer, n_lat_tiles, tiles_per_seq, n_batch, d),
                  tok(w_a), tok(w_b),
                  pl.BlockSpec((HALO, w_b), lambda i: (jnp.maximum(i * hb - 1, 0), 0)),
                  pl.BlockSpec((HALO, w_b), lambda i: (jnp.minimum((i + 1) * hb, last_halo), 0)),
                  pl.BlockSpec((w_c, tm), lambda i: (0, i))] + [_const_spec(c.shape) for c in consts],
        out_specs=tok(d),
        out_shape=jax.ShapeDtypeStruct((n_tok, d), F32),
        scratch_shapes=[pltpu.VMEM((tm + 2 * HALO, w_b), F32), pltpu.VMEM((tm, w_b), F32)],
        compiler_params=_params("parallel"),
        name="tail",
    )(x_parts[0][0], x_parts[1][0], mod, ya, bpool, bpool, bpool, o, *consts)


def _rope_tables(seq):
    rows = seq // GRID_W
    row = jnp.repeat(jnp.arange(rows), GRID_W).astype(F32)
    col = jnp.tile(jnp.arange(GRID_W), rows).astype(F32)
    n_freq = QK_ROPE // 4
    inv = ROPE_BASE ** (-jnp.arange(n_freq, dtype=F32) / n_freq)
    ang = jnp.concatenate([row[:, None] * inv, col[:, None] * inv], axis=-1)
    cos = jnp.repeat(jnp.cos(ang), 2, axis=-1)
    sin = jnp.repeat(jnp.sin(ang), 2, axis=-1) * jnp.tile(jnp.array([-1.0, 1.0], F32), QK_ROPE // 2)
    pad = LANES - QK_HEAD
    cos_t = jnp.concatenate([jnp.ones((seq, QK_NOPE), F32), cos, jnp.zeros((seq, pad), F32)], axis=-1)
    sin_t = jnp.concatenate([jnp.zeros((seq, QK_NOPE), F32), sin, jnp.zeros((seq, pad), F32)], axis=-1)
    ident_c = jnp.concatenate([jnp.ones((TM_PRE, QK_HEAD), F32), jnp.zeros((TM_PRE, pad), F32)], axis=-1)
    ident_s = jnp.zeros((TM_PRE, LANES), F32)
    return jnp.concatenate([cos_t, ident_c], axis=0), jnp.concatenate([sin_t, ident_s], axis=0)


def _layer_weights(l, norm1_g, norm2_g, w_in, sgu_norm_g, w_spatial, b_spatial, w_pool, pool_scale,
                   q_a_norm_g, w_q_b, kv_a_norm_g, w_kv_b, q_norm_g, k_norm_g, w_out, w_gate_up,
                   w_down):
    d = w_in.shape[1]
    w_a = sgu_norm_g.shape[1]
    w_b = pool_scale.shape[1]
    n_heads = w_q_b.shape[2] // QK_HEAD
    pad = LANES - QK_HEAD
    split = 2 * w_a + w_b + Q_RANK + KV_RANK
    wi = w_in[l]
    w_in_p = jnp.concatenate([wi[:, :split], jnp.zeros((d, QK_NOPE), F32), wi[:, split:],
                              jnp.zeros((d, pad), F32)], axis=1)
    wq = w_q_b[l].reshape(Q_RANK, n_heads, QK_HEAD)
    q_gain = q_norm_g[l] * np.float32(QK_HEAD ** -0.5 * np.log2(np.e))
    wq_rot = (wq[:, :, QK_NOPE:] * q_gain[QK_NOPE:]).reshape(Q_RANK, n_heads, QK_ROPE // 2, 2)
    wq_swap = wq_rot[..., ::-1].reshape(Q_RANK, n_heads, QK_ROPE)
    w_qb = jnp.concatenate([
        jnp.pad(wq, ((0, 0), (0, 0), (0, pad))).reshape(Q_RANK, -1),
        jnp.pad(wq_swap, ((0, 0), (0, 0), (QK_NOPE, pad))).reshape(Q_RANK, -1)], axis=1)
    kvb = w_kv_b[l].reshape(KV_RANK, n_heads, QK_NOPE + V_DIM)
    w_kb = jnp.pad(kvb[:, :, :QK_NOPE], ((0, 0), (0, 0), (0, LANES - QK_NOPE)))
    w_vb = jnp.pad(kvb[:, :, QK_NOPE:], ((0, 0), (0, 0), (0, LANES - V_DIM)))
    vone = jnp.zeros((LANES, TM_PRE), F32).at[V_DIM, :].set(1.0)
    return dict(
        g1=norm1_g[l][None, :], g2=norm2_g[l][None, :],
        w_in=w_in_p.astype(BF16),
        sgu_g=sgu_norm_g[l][None, :],
        w_sp=w_spatial[l].reshape(A_HEADS * CHUNK, CHUNK).astype(BF16),
        b_sp=jnp.repeat(b_spatial[l].T, w_a // A_HEADS, axis=1),
        qa_g=q_a_norm_g[l][None, :], kva_g=kv_a_norm_g[l][None, :],
        w_qb=w_qb.astype(BF16),
        w_kb=w_kb.reshape(KV_RANK, -1).astype(BF16),
        w_vbt=w_vb.reshape(KV_RANK, -1).T.astype(BF16),
        qg=jnp.pad(q_gain, (0, pad))[None, :],
        kg=jnp.pad(k_norm_g[l], (0, pad))[None, :],
        vone=vone,
        w_out=w_out[l].astype(BF16),
        w_pool=jax.scipy.linalg.block_diag(*[w_pool[l, g] for g in range(w_pool.shape[1])]).astype(BF16),
        pool_scale=pool_scale[l][None, :],
        w_gu=w_gate_up[l].astype(BF16), w_dn=w_down[l].astype(BF16),
    )


def kernel(x, c, ctx, c_ctx, norm1_g, norm2_g, w_ada, b_ada, w_in, sgu_norm_g, w_spatial, b_spatial,
           w_pool, pool_scale, q_a_norm_g, w_q_b, kv_a_norm_g, w_kv_b, q_norm_g, k_norm_g, w_out,
           w_gate_up, w_down):
    n_batch, seq, d = x.shape
    ctx_len = ctx.shape[1]
    depth = w_in.shape[0]
    n_lat = n_batch * seq
    n_ctx = n_batch * ctx_len
    assert seq % GRID_W == 0
    for tm in (TM_PRE, TM_TAIL):
        assert seq % tm == 0 and n_ctx % tm == 0

    x_parts = ((x.reshape(n_lat, d), 0), (ctx.reshape(n_ctx, d), 0))
    n_rows = -(-(n_batch + 1) // SUBLANES) * SUBLANES
    cc = jnp.concatenate([c, c_ctx[None, :], jnp.zeros((n_rows - n_batch - 1, d), F32)], axis=0)
    mod = _mod_call(cc, w_ada, b_ada).reshape(depth, n_rows, N_MOD, d)
    cos_tab, sin_tab = _rope_tables(seq)

    for l in range(depth):
        wts = _layer_weights(l, norm1_g, norm2_g, w_in, sgu_norm_g, w_spatial, b_spatial, w_pool,
                             pool_scale, q_a_norm_g, w_q_b, kv_a_norm_g, w_kv_b, q_norm_g, k_norm_g,
                             w_out, w_gate_up, w_down)
        with_ctx = l < depth - 1
        n_tok = n_lat + n_ctx if with_ctx else n_lat
        ya, bpool, q, k, vt = _pre_call(x_parts, mod, l, wts, cos_tab, sin_tab, n_lat=n_lat,
                                       n_ctx=n_ctx, seq=seq, n_batch=n_batch, ctx_len=ctx_len)
        o = _attn_call(q, k, vt, n_batch=n_batch, seq=seq, ctx_len=ctx_len, with_ctx=with_ctx)
        x_all = _tail_call(x_parts, mod, l, ya, bpool, o, wts, n_tok=n_tok, n_lat=n_lat,
                           n_batch=n_batch, seq=seq, ctx_len=ctx_len)
        x_parts = ((x_all, 0), (x_all, n_lat))
    return x_all[:n_lat].reshape(n_batch, seq, d)
```

```python
import functools

import jax
import jax.numpy as jnp
import numpy as np
from jax import lax
from jax.experimental import pallas as pl
from jax.experimental.pallas import tpu as pltpu

F32 = jnp.float32
BF16 = jnp.bfloat16

GRID_W = 64
CHUNK = 128
A_HEADS = 4
POOL_WINDOWS = (2, 4, 8, 16)
V_DIM = 64
QK_NOPE = 64
QK_ROPE = 32
QK_HEAD = QK_NOPE + QK_ROPE
Q_RANK = 256
KV_RANK = 128
ROPE_BASE = 10000.0
EPS = 1e-6
N_MOD = 6

LANES = 128
SUBLANES = 8
HALO = 8
VMEM_LIMIT = 56 * 1024 * 1024

TM_PRE = 256
TM_TAIL = 512
TQ = 256
FF_CHUNK = 256
N_KV_BLOCKS = 2
HEADS_PER_GROUP = 2
PV_ROWS = 80


def _rms(x, g):
    return x * lax.rsqrt(jnp.mean(x * x, axis=-1, keepdims=True) + EPS) * g


def _params(*sem):
    return pltpu.CompilerParams(dimension_semantics=sem, vmem_limit_bytes=VMEM_LIMIT)


def _const_spec(shape, layer=None):
    nd = len(shape)
    if layer is None:
        return pl.BlockSpec(shape, lambda *_: (0,) * nd, pipeline_mode=pl.Buffered(1))
    return pl.BlockSpec((None,) + tuple(shape), lambda *_: (layer,) + (0,) * nd, pipeline_mode=pl.Buffered(1))


def _identity(i):
    return i


def _token_specs(x_parts, tm, n_lat_tiles, tile_of_step=_identity):
    (xa, row_a), (xb, row_b) = x_parts
    assert row_a % tm == 0 and row_b % tm == 0
    d = xa.shape[1]
    return [pl.BlockSpec((tm, d), lambda s: (row_a // tm + jnp.minimum(tile_of_step(s), n_lat_tiles - 1), 0)),
            pl.BlockSpec((tm, d), lambda s: (row_b // tm + jnp.maximum(tile_of_step(s) - n_lat_tiles, 0), 0))]


def _load_tokens(xa_ref, xb_ref, tile, n_lat_tiles):
    return jnp.where(tile < n_lat_tiles, xa_ref[...], xb_ref[...])


def _mod_spec(layer, n_lat_tiles, tiles_per_seq, n_batch, d, tile_of_step=_identity):
    def index(s):
        i = tile_of_step(s)
        return (layer, jnp.where(i < n_lat_tiles, i // tiles_per_seq, n_batch), 0, 0)
    return pl.BlockSpec((None, None, N_MOD, d), index)


def _mod_kernel(c_ref, w_ref, b_ref, o_ref):
    c = c_ref[...]
    s = c * jax.nn.sigmoid(c)
    o_ref[...] = jnp.dot(s, w_ref[...], preferred_element_type=F32) + b_ref[...]


def _mod_call(cc, w_ada, b_ada):
    n_layer, d, n_out = w_ada.shape
    r = cc.shape[0]
    tn = 1024
    return pl.pallas_call(
        _mod_kernel,
        grid=(n_layer, n_out // tn),
        in_specs=[
            pl.BlockSpec((r, d), lambda l, j: (0, 0)),
            pl.BlockSpec((None, d, tn), lambda l, j: (l, 0, j)),
            pl.BlockSpec((None, 1, tn), lambda l, j: (l, 0, j)),
        ],
        out_specs=pl.BlockSpec((None, r, tn), lambda l, j: (l, 0, j)),
        out_shape=jax.ShapeDtypeStruct((n_layer, r, n_out), F32),
        compiler_params=_params("parallel", "parallel"),
        name="mod",
    )(cc, w_ada, b_ada.reshape(n_layer, 1, n_out))


def _swap_pairs(x):
    lane = lax.broadcasted_iota(jnp.int32, x.shape, 1)
    nxt = pltpu.roll(x, x.shape[1] - 1, 1)
    prv = pltpu.roll(x, 1, 1)
    return jnp.where(lane % 2 == 0, nxt, prv)


def _pre_kernel(xa_ref, xb_ref, mod_ref, g1_ref, w_in_ref, sgu_g_ref, w_sp_ref, b_sp_ref,
                qa_g_ref, w_qb_ref, kva_g_ref, w_kb_ref, w_vbt_ref, qg_ref, kg_ref, vone_ref,
                cos_ref, sin_ref,
                ya_ref, b_ref, q_ref, k_ref, vt_ref, *, w_a, w_b, n_heads, n_lat_tiles):
    x = _load_tokens(xa_ref, xb_ref, pl.program_id(0), n_lat_tiles)
    sh1 = mod_ref[0:1, :]
    sc1 = mod_ref[1:2, :]
    h = _rms(x, g1_ref[...]) * (1.0 + sc1) + sh1
    p = jnp.dot(h.astype(BF16), w_in_ref[...], preferred_element_type=F32)

    a = p[:, :2 * w_a]
    ga = 0.5 * a * (1.0 + lax.erf(a * np.float32(1.0 / np.sqrt(2.0))))
    u = ga[:, :w_a]
    vg = _rms(ga[:, w_a:], sgu_g_ref[...]).astype(BF16)
    head_of_lane = lax.broadcasted_iota(jnp.int32, (CHUNK, w_a), 1) // (w_a // A_HEADS)
    for c in range(x.shape[0] // CHUNK):
        rows = slice(c * CHUNK, (c + 1) * CHUNK)
        r = jnp.dot(w_sp_ref[...], vg[rows, :], preferred_element_type=F32)
        mixed = r[0:CHUNK, :]
        for hh in range(1, A_HEADS):
            mixed = jnp.where(head_of_lane == hh, r[hh * CHUNK:(hh + 1) * CHUNK, :], mixed)
        ya_ref[rows, :] = (u[rows, :] * (mixed + b_sp_ref[...])).astype(ya_ref.dtype)

    o = 2 * w_a
    b_ref[...] = p[:, o:o + w_b]

    o += w_b
    pqn = _rms(p[:, o:o + Q_RANK], qa_g_ref[...]).astype(BF16)
    o += Q_RANK
    pkvn = _rms(p[:, o:o + KV_RANK], kva_g_ref[...]).astype(BF16)
    o += KV_RANK
    kr = p[:, o:o + LANES]
    hw = n_heads * LANES
    q2 = jnp.dot(pqn, w_qb_ref[...], preferred_element_type=F32)
    k_raw = jnp.dot(pkvn, w_kb_ref[...], preferred_element_type=F32)
    cos = cos_ref[...]
    sin = sin_ref[...]
    gcos = cos * qg_ref[...]
    krg = kr * kg_ref[...]
    kr_rot = krg * cos + _swap_pairs(krg) * sin
    ss_kr = jnp.sum(kr * kr, axis=-1, keepdims=True)
    for hh in range(n_heads):
        cols = slice(hh * LANES, (hh + 1) * LANES)
        qh = q2[:, cols]
        rq = lax.rsqrt(jnp.sum(qh * qh, axis=-1, keepdims=True) * (1.0 / QK_HEAD) + EPS)
        q_ref[hh] = (rq * (qh * gcos + q2[:, hw + hh * LANES:hw + (hh + 1) * LANES] * sin)).astype(q_ref.dtype)
        kh = k_raw[:, cols]
        rk = lax.rsqrt((jnp.sum(kh * kh, axis=-1, keepdims=True) + ss_kr) * (1.0 / QK_HEAD) + EPS)
        k_ref[hh] = (rk * (kh * kg_ref[...] + kr_rot)).astype(k_ref.dtype)
    vt = lax.dot_general(w_vbt_ref[...], pkvn, (((1,), (1,)), ((), ())), preferred_element_type=F32)
    for hh in range(n_heads):
        vt_ref[hh] = (vt[hh * LANES:(hh + 1) * LANES, :] + vone_ref[...]).astype(vt_ref.dtype)


def _pre_call(x_parts, mod, layer, wts, cos_tab, sin_tab, *, n_lat, n_ctx, seq, n_batch, ctx_len):
    nt, d = n_lat + n_ctx, x_parts[0][0].shape[1]
    tm = TM_PRE
    n_lat_tiles = n_lat // tm
    tiles_per_seq = seq // tm
    w_a = wts["sgu_g"].shape[1]
    w_b = wts["pool_scale"].shape[1]
    n_heads = wts["w_kb"].shape[1] // LANES
    assert ctx_len == tm

    def ridx(i):
        return jnp.where(i < n_lat_tiles, i % tiles_per_seq, tiles_per_seq)

    tok = lambda w: pl.BlockSpec((tm, w), lambda i: (i, 0))
    consts = [wts[k] for k in ("g1", "w_in", "sgu_g", "w_sp", "b_sp", "qa_g", "w_qb", "kva_g",
                               "w_kb", "w_vbt", "qg", "kg", "vone")]
    in_specs = _token_specs(x_parts, tm, n_lat_tiles)
    in_specs += [_mod_spec(layer, n_lat_tiles, tiles_per_seq, n_batch, d)]
    in_specs += [_const_spec(c.shape) for c in consts]
    in_specs += [pl.BlockSpec((tm, LANES), lambda i: (ridx(i), 0))] * 2
    out_shape = [jax.ShapeDtypeStruct((nt, w_a), BF16), jax.ShapeDtypeStruct((nt, w_b), F32),
                 jax.ShapeDtypeStruct((n_heads, nt, LANES), BF16),
                 jax.ShapeDtypeStruct((n_heads, nt, LANES), BF16),
                 jax.ShapeDtypeStruct((n_heads, LANES, nt), BF16)]
    head_tok = pl.BlockSpec((n_heads, tm, LANES), lambda i: (0, i, 0))

    def key_tile(i):
        lat = (i // tiles_per_seq) * (tiles_per_seq + 1) + 1 + i % tiles_per_seq
        return jnp.where(i < n_lat_tiles, lat, (i - n_lat_tiles) * (tiles_per_seq + 1))
    return pl.pallas_call(
        functools.partial(_pre_kernel, w_a=w_a, w_b=w_b, n_heads=n_heads, n_lat_tiles=n_lat_tiles),
        grid=(nt // tm,),
        in_specs=in_specs,
        out_specs=[tok(w_a), tok(w_b), head_tok,
                   pl.BlockSpec((n_heads, tm, LANES), lambda i: (0, key_tile(i), 0)),
                   pl.BlockSpec((n_heads, LANES, tm), lambda i: (0, 0, key_tile(i)))],
        out_shape=out_shape,
        compiler_params=_params("parallel"),
        name="pre",
    )(x_parts[0][0], x_parts[1][0], mod, *consts, cos_tab, sin_tab)


def _when(cond, fn):
    if isinstance(cond, bool):
        if cond:
            fn()
    else:
        pl.when(cond)(fn)


def _attend(q_ref, qn_ref, k_ref, vt_ref, ot_ref, s_ref, acc_ref, mx_ref, n_keys, n_heads,
            is_first, has_next):
    tq = q_ref.shape[1]
    grp = HEADS_PER_GROUP
    n_groups = n_heads // grp
    assert n_groups % 2 == 0
    n_blocks = N_KV_BLOCKS if n_keys % (N_KV_BLOCKS * LANES) == 0 else 1
    blk = n_keys // n_blocks

    def stage(g_exp, g_qk, qsrc_ref, st, ms, mxs):
        rows = pl.ds(st, blk)
        mxs = list(mxs)
        for g in range(grp):
            if g_exp is not None:
                hh = g_exp * grp + g
                p = jnp.exp2(s_ref[(g_exp % 2) * grp + g, rows, :] - ms[g]).astype(BF16)
                acc_ref[g] += jnp.dot(vt_ref[hh, 0:PV_ROWS, rows], p,
                                      preferred_element_type=F32)
            if g_qk is not None:
                hn = g_qk * grp + g
                sn = lax.dot_general(k_ref[hn, rows, :], qsrc_ref[hn], (((1,), (1,)), ((), ())),
                                     preferred_element_type=F32)
                s_ref[(g_qk % 2) * grp + g, rows, :] = sn
                mxs[g] = jnp.maximum(mxs[g], jnp.max(sn.reshape(blk // SUBLANES, SUBLANES, tq), axis=0))
        return tuple(mxs)

    def sweep(g_exp, g_qk, qsrc_ref):
        ms = None
        if g_exp is not None:
            ms = [jnp.max(mx_ref[g], axis=0, keepdims=True) for g in range(grp)]
            acc_ref[...] = jnp.zeros(acc_ref.shape, F32)
        mxs = tuple(jnp.full((SUBLANES, tq), -jnp.inf, F32) for _ in range(grp))
        if n_blocks == 1:
            mxs = stage(g_exp, g_qk, qsrc_ref, 0, ms, mxs)
        else:
            mxs = lax.fori_loop(
                0, n_blocks,
                lambda j, c: stage(g_exp, g_qk, qsrc_ref, pl.multiple_of(j * blk, LANES), ms, c), mxs)
        if g_qk is not None:
            for g in range(grp):
                mx_ref[g] = mxs[g]

    def write_out(gi):
        for g in range(grp):
            out = acc_ref[g, 0:V_DIM, :] / acc_ref[g, V_DIM:V_DIM + 1, :]
            hh = gi * grp + g
            ot_ref[hh * V_DIM:(hh + 1) * V_DIM, :] = out.astype(ot_ref.dtype)

    _when(is_first, lambda: sweep(None, 0, q_ref))
    for gi in range(n_groups - 1):
        sweep(gi, gi + 1, q_ref)
        write_out(gi)
    last = n_groups - 1
    _when(has_next, lambda: sweep(last, 0, qn_ref))
    _when(jnp.logical_not(has_next) if not isinstance(has_next, bool) else not has_next,
          lambda: sweep(last, None, None))
    write_out(last)


def _attn_kernel(q_ref, qn_ref, k_ref, vt_ref, ot_ref, s_ref, acc_ref, mx_ref, *, nq, ctx_len, n_heads,
                 with_ctx):
    n_keys = k_ref.shape[1]
    i = pl.program_id(1)
    args = (q_ref, qn_ref, k_ref, vt_ref, ot_ref, s_ref, acc_ref, mx_ref)
    if not with_ctx:
        _attend(*args, n_keys, n_heads, i == 0, i < nq - 1)
        return

    @pl.when(i < nq)
    def _():
        _attend(*args, n_keys, n_heads, i == 0, i < nq - 1)

    @pl.when(i == nq)
    def _():
        _attend(*args, ctx_len, n_heads, True, False)


def _attn_call(q, k, vt, *, n_batch, seq, ctx_len, with_ctx):
    n_heads, nt, _ = q.shape
    assert ctx_len == TQ and n_heads % HEADS_PER_GROUP == 0
    nq = seq // TQ
    n_lat = n_batch * seq
    n_keys = seq + ctx_len

    def q_tile(b, i):
        return jnp.where(i < nq, b * nq + i, n_lat // TQ + b)

    one = pl.Buffered(1)
    return pl.pallas_call(
        functools.partial(_attn_kernel, nq=nq, ctx_len=ctx_len, n_heads=n_heads, with_ctx=with_ctx),
        grid=(n_batch, nq + (1 if with_ctx else 0)),
        in_specs=[pl.BlockSpec((n_heads, TQ, LANES), lambda b, i: (0, q_tile(b, i), 0)),
                  pl.BlockSpec((n_heads, TQ, LANES), lambda b, i: (0, b * nq + jnp.minimum(i + 1, nq - 1), 0)),
                  pl.BlockSpec((n_heads, n_keys, LANES), lambda b, i: (0, b, 0)),
                  pl.BlockSpec((n_heads, LANES, n_keys), lambda b, i: (0, 0, b), pipeline_mode=one)],
        out_specs=pl.BlockSpec((n_heads * V_DIM, TQ), lambda b, i: (0, q_tile(b, i))),
        out_shape=jax.ShapeDtypeStruct((n_heads * V_DIM, nt if with_ctx else n_lat), BF16),
        scratch_shapes=[pltpu.VMEM((2 * HEADS_PER_GROUP, n_keys, TQ), F32),
                        pltpu.VMEM((HEADS_PER_GROUP, PV_ROWS, TQ), F32),
                        pltpu.VMEM((HEADS_PER_GROUP, SUBLANES, TQ), F32)],
        compiler_params=_params("arbitrary", "arbitrary"),
        name="attn",
    )(q, q, k, vt)


def _tail_kernel(xa_ref, xb_ref, moda_ref, modb_ref, ya_ref, b_ref, bp_ref, bn_ref, ot_ref, w_out_ref,
                 w_pool_ref, ps_ref, g2_ref, w_gu_ref, w_dn_ref, xo_ref, e_ref, s2_ref, s4_ref, s8_ref,
                 x1_ref, h_ref, *,
                 n_tiles, n_lat_tiles, tiles_per_seq, seq, ctx_len, d_ff):
    s = pl.program_id(0)
    tm, w_b = b_ref.shape
    sum_refs = (e_ref, s2_ref, s4_ref, s8_ref)

    def mix_stage(slot):
        tile = jnp.minimum(s, n_tiles - 1)
        is_lat = tile < n_lat_tiles
        n_seq = jnp.where(is_lat, seq, ctx_len)
        seg = ctx_len
        ext = seg + 2 * HALO
        group = lax.broadcasted_iota(jnp.int32, (seg, w_b), 1) // (w_b // len(POOL_WINDOWS))
        zero_halo = jnp.zeros((HALO, w_b), F32)
        for ref in sum_refs:
            ref[ext:, :] = zero_halo
        parts = []
        for c in range(tm // seg):
            r0 = c * seg
            pos0 = jnp.where(is_lat, (tile % tiles_per_seq) * tm + r0, 0)
            prev = bp_ref[...] if c == 0 else b_ref[r0 - HALO:r0, :]
            nxt = bn_ref[...] if r0 + seg == tm else b_ref[r0 + seg:r0 + seg + HALO, :]
            e_ref, s2_ref, s4_ref, s8_ref = sum_refs
            e_ref[0:HALO, :] = jnp.where(pos0 > 0, prev, zero_halo)
            e_ref[HALO:HALO + seg, :] = b_ref[r0:r0 + seg, :]
            e_ref[HALO + seg:ext, :] = jnp.where(pos0 + seg < n_seq, nxt, zero_halo)
            s2_ref[0:ext, :] = e_ref[0:ext, :] + e_ref[1:ext + 1, :]
            s4_ref[0:ext, :] = s2_ref[0:ext, :] + s2_ref[2:ext + 2, :]
            s8_ref[0:ext, :] = s4_ref[0:ext, :] + s4_ref[4:ext + 4, :]
            wins = [s2_ref[HALO - 1:HALO - 1 + seg, :], s4_ref[HALO - 2:HALO - 2 + seg, :],
                    s8_ref[HALO - 4:HALO - 4 + seg, :], s8_ref[0:seg, :] + s8_ref[HALO:HALO + seg, :]]
            pos = pos0 + lax.broadcasted_iota(jnp.int32, (seg, 1), 0)
            win = inv = None
            for g, w in enumerate(POOL_WINDOWS):
                half = w // 2
                cnt = jnp.minimum(pos + half, n_seq) - jnp.maximum(pos - half, 0)
                inv_g = 1.0 / cnt.astype(F32)
                win = wins[g] if win is None else jnp.where(group == g, wins[g], win)
                inv = inv_g if inv is None else jnp.where(group == g, inv_g, inv)
            parts.append(win * inv - b_ref[r0:r0 + seg, :])
        dlt = jnp.concatenate(parts, axis=0)
        yb = jnp.dot(dlt.astype(BF16), w_pool_ref[...], preferred_element_type=F32) * ps_ref[...]

        w_ab = ya_ref.shape[1] + w_b
        y_ab = jnp.concatenate([ya_ref[...], yb.astype(BF16)], axis=-1)
        mix = jnp.dot(y_ab, w_out_ref[0:w_ab, :], preferred_element_type=F32)
        mix += lax.dot_general(ot_ref[...], w_out_ref[w_ab:, :], (((0,), (0,)), ((), ())),
                               preferred_element_type=F32)
        x1 = _load_tokens(xa_ref, xb_ref, tile, n_lat_tiles) + moda_ref[2:3, :] * mix
        x1_ref[slot] = x1
        h = _rms(x1, g2_ref[...]) * (1.0 + moda_ref[4:5, :]) + moda_ref[3:4, :]
        h_ref[slot] = h.astype(BF16)

    def ffn_stage(slot):
        h = h_ref[slot]
        x1 = x1_ref[slot]
        fc = FF_CHUNK
        y = None
        for j in range(d_ff // fc):
            gt = jnp.dot(h, w_gu_ref[:, j * fc:(j + 1) * fc], preferred_element_type=F32)
            up = jnp.dot(h, w_gu_ref[:, d_ff + j * fc:d_ff + (j + 1) * fc], preferred_element_type=F32)
            t = (gt * jax.nn.sigmoid(gt) * up).astype(BF16)
            part = jnp.dot(t, w_dn_ref[j * fc:(j + 1) * fc, :], preferred_element_type=F32)
            y = part if y is None else y + part
        xo_ref[...] = x1 + modb_ref[5:6, :] * y

    @pl.when(s == 0)
    def _():
        mix_stage(0)

    @pl.when(jnp.logical_and(s > 0, s < n_tiles))
    def _():
        ffn_stage((s - 1) % 2)
        mix_stage(s % 2)

    @pl.when(s == n_tiles)
    def _():
        ffn_stage((s - 1) % 2)


def _tail_call(x_parts, mod, layer, ya, bpool, o, wts, *, n_tok, n_lat, n_batch, seq, ctx_len):
    nt, d = bpool.shape[0], x_parts[0][0].shape[1]
    tm = TM_TAIL
    assert tm % ctx_len == 0 and n_tok % tm == 0
    n_tiles = n_tok // tm
    n_lat_tiles = n_lat // tm
    tiles_per_seq = seq // tm
    w_a = ya.shape[1]
    w_b = bpool.shape[1]
    w_c = o.shape[0]
    d_ff = wts["w_dn"].shape[1]
    assert d_ff % FF_CHUNK == 0
    assert POOL_WINDOWS == (2, 4, 8, 16) and HALO == POOL_WINDOWS[-1] // 2
    hb = tm // HALO
    last_halo = nt // HALO - 1

    def mix_tile(s):
        return jnp.minimum(s, n_tiles - 1)

    def ffn_tile(s):
        return jnp.maximum(s - 1, 0)

    tok = lambda w: pl.BlockSpec((tm, w), lambda s: (mix_tile(s), 0))
    consts = [wts[k] for k in ("w_out", "w_pool", "pool_scale", "g2", "w_gu", "w_dn")]
    const_specs = [_const_spec(c.shape[1:], layer) if k in ("w_out", "w_gu", "w_dn") else _const_spec(c.shape)
                   for k, c in zip(("w_out", "w_pool", "pool_scale", "g2", "w_gu", "w_dn"), consts)]
    return pl.pallas_call(
        functools.partial(_tail_kernel, n_tiles=n_tiles, n_lat_tiles=n_lat_tiles,
                          tiles_per_seq=tiles_per_seq, seq=seq, ctx_len=ctx_len, d_ff=d_ff),
        grid=(n_tiles + 1,),
        in_specs=_token_specs(x_parts, tm, n_lat_tiles, mix_tile) + [
                  _mod_spec(layer, n_lat_tiles, tiles_per_seq, n_batch, d, mix_tile),
                  _mod_spec(layer, n_lat_tiles, tiles_per_seq, n_batch, d, ffn_tile),
                  tok(w_a), tok(w_b),
                  pl.BlockSpec((HALO, w_b), lambda s: (jnp.maximum(mix_tile(s) * hb - 1, 0), 0)),
                  pl.BlockSpec((HALO, w_b), lambda s: (jnp.minimum((mix_tile(s) + 1) * hb, last_halo), 0)),
                  pl.BlockSpec((w_c, tm), lambda s: (0, mix_tile(s)))] + const_specs,
        out_specs=pl.BlockSpec((tm, d), lambda s: (ffn_tile(s), 0)),
        out_shape=jax.ShapeDtypeStruct((n_tok, d), F32),
        scratch_shapes=[pltpu.VMEM((ctx_len + 3 * HALO, w_b), F32)] * 4 + [
                        pltpu.VMEM((2, tm, d), F32), pltpu.VMEM((2, tm, d), BF16)],
        compiler_params=_params("arbitrary"),
        name="tail",
    )(x_parts[0][0], x_parts[1][0], mod, mod, ya, bpool, bpool, bpool, o, *consts)


def _rope_tables(seq):
    rows = seq // GRID_W
    row = jnp.repeat(jnp.arange(rows), GRID_W).astype(F32)
    col = jnp.tile(jnp.arange(GRID_W), rows).astype(F32)
    n_freq = QK_ROPE // 4
    inv = ROPE_BASE ** (-jnp.arange(n_freq, dtype=F32) / n_freq)
    ang = jnp.concatenate([row[:, None] * inv, col[:, None] * inv], axis=-1)
    cos = jnp.repeat(jnp.cos(ang), 2, axis=-1)
    sin = jnp.repeat(jnp.sin(ang), 2, axis=-1) * jnp.tile(jnp.array([-1.0, 1.0], F32), QK_ROPE // 2)
    pad = LANES - QK_HEAD
    cos_t = jnp.concatenate([jnp.ones((seq, QK_NOPE), F32), cos, jnp.zeros((seq, pad), F32)], axis=-1)
    sin_t = jnp.concatenate([jnp.zeros((seq, QK_NOPE), F32), sin, jnp.zeros((seq, pad), F32)], axis=-1)
    ident_c = jnp.concatenate([jnp.ones((TM_PRE, QK_HEAD), F32), jnp.zeros((TM_PRE, pad), F32)], axis=-1)
    ident_s = jnp.zeros((TM_PRE, LANES), F32)
    return jnp.concatenate([cos_t, ident_c], axis=0), jnp.concatenate([sin_t, ident_s], axis=0)


def _layer_weights(l, norm1_g, norm2_g, w_in, sgu_norm_g, w_spatial, b_spatial, w_pool, pool_scale,
                   q_a_norm_g, w_q_b, kv_a_norm_g, w_kv_b, q_norm_g, k_norm_g, w_out, w_gate_up,
                   w_down):
    d = w_in.shape[1]
    w_a = sgu_norm_g.shape[1]
    w_b = pool_scale.shape[1]
    n_heads = w_q_b.shape[2] // QK_HEAD
    pad = LANES - QK_HEAD
    split = 2 * w_a + w_b + Q_RANK + KV_RANK
    wi = w_in[l]
    w_in_p = jnp.concatenate([wi[:, :split], jnp.zeros((d, QK_NOPE), F32), wi[:, split:],
                              jnp.zeros((d, pad), F32)], axis=1)
    wq = w_q_b[l].reshape(Q_RANK, n_heads, QK_HEAD)
    q_gain = q_norm_g[l] * np.float32(QK_HEAD ** -0.5 * np.log2(np.e))
    wq_rot = (wq[:, :, QK_NOPE:] * q_gain[QK_NOPE:]).reshape(Q_RANK, n_heads, QK_ROPE // 2, 2)
    wq_swap = wq_rot[..., ::-1].reshape(Q_RANK, n_heads, QK_ROPE)
    w_qb = jnp.concatenate([
        jnp.pad(wq, ((0, 0), (0, 0), (0, pad))).reshape(Q_RANK, -1),
        jnp.pad(wq_swap, ((0, 0), (0, 0), (QK_NOPE, pad))).reshape(Q_RANK, -1)], axis=1)
    kvb = w_kv_b[l].reshape(KV_RANK, n_heads, QK_NOPE + V_DIM)
    w_kb = jnp.pad(kvb[:, :, :QK_NOPE], ((0, 0), (0, 0), (0, LANES - QK_NOPE)))
    w_vb = jnp.pad(kvb[:, :, QK_NOPE:], ((0, 0), (0, 0), (0, LANES - V_DIM)))
    vone = jnp.zeros((LANES, TM_PRE), F32).at[V_DIM, :].set(1.0)
    return dict(
        g1=norm1_g[l][None, :], g2=norm2_g[l][None, :],
        w_in=w_in_p.astype(BF16),
        sgu_g=sgu_norm_g[l][None, :],
        w_sp=w_spatial[l].reshape(A_HEADS * CHUNK, CHUNK).astype(BF16),
        b_sp=jnp.repeat(b_spatial[l].T, w_a // A_HEADS, axis=1),
        qa_g=q_a_norm_g[l][None, :], kva_g=kv_a_norm_g[l][None, :],
        w_qb=w_qb.astype(BF16),
        w_kb=w_kb.reshape(KV_RANK, -1).astype(BF16),
        w_vbt=w_vb.reshape(KV_RANK, -1).T.astype(BF16),
        qg=jnp.pad(q_gain, (0, pad))[None, :],
        kg=jnp.pad(k_norm_g[l], (0, pad))[None, :],
        vone=vone,
        w_out=w_out.astype(BF16),
        w_pool=jax.scipy.linalg.block_diag(*[w_pool[l, g] for g in range(w_pool.shape[1])]).astype(BF16),
        pool_scale=pool_scale[l][None, :],
        w_gu=w_gate_up.astype(BF16), w_dn=w_down.astype(BF16),
    )


def kernel(x, c, ctx, c_ctx, norm1_g, norm2_g, w_ada, b_ada, w_in, sgu_norm_g, w_spatial, b_spatial,
           w_pool, pool_scale, q_a_norm_g, w_q_b, kv_a_norm_g, w_kv_b, q_norm_g, k_norm_g, w_out,
           w_gate_up, w_down):
    n_batch, seq, d = x.shape
    ctx_len = ctx.shape[1]
    depth = w_in.shape[0]
    n_lat = n_batch * seq
    n_ctx = n_batch * ctx_len
    assert seq % GRID_W == 0
    for tm in (TM_PRE, TM_TAIL):
        assert seq % tm == 0 and n_ctx % tm == 0

    x_parts = ((x.reshape(n_lat, d), 0), (ctx.reshape(n_ctx, d), 0))
    n_rows = -(-(n_batch + 1) // SUBLANES) * SUBLANES
    cc = jnp.concatenate([c, c_ctx[None, :], jnp.zeros((n_rows - n_batch - 1, d), F32)], axis=0)
    mod = _mod_call(cc, w_ada, b_ada).reshape(depth, n_rows, N_MOD, d)
    cos_tab, sin_tab = _rope_tables(seq)

    for l in range(depth):
        wts = _layer_weights(l, norm1_g, norm2_g, w_in, sgu_norm_g, w_spatial, b_spatial, w_pool,
                             pool_scale, q_a_norm_g, w_q_b, kv_a_norm_g, w_kv_b, q_norm_g, k_norm_g,
                             w_out, w_gate_up, w_down)
        with_ctx = l < depth - 1
        n_tok = n_lat + n_ctx if with_ctx else n_lat
        ya, bpool, q, k, vt = _pre_call(x_parts, mod, l, wts, cos_tab, sin_tab, n_lat=n_lat,
                                       n_ctx=n_ctx, seq=seq, n_batch=n_batch, ctx_len=ctx_len)
        o = _attn_call(q, k, vt, n_batch=n_batch, seq=seq, ctx_len=ctx_len, with_ctx=with_ctx)
        x_all = _tail_call(x_parts, mod, l, ya, bpool, o, wts, n_tok=n_tok, n_lat=n_lat,
                           n_batch=n_batch, seq=seq, ctx_len=ctx_len)
        x_parts = ((x_all, 0), (x_all, n_lat))
    return x_all[:n_lat].reshape(n_batch, seq, d)
```
